```python
import jax
import jax.numpy as jnp
from jax import lax
import numpy as np

D_MODEL = 2048
BATCH = 2
SEQ = 8192
DEPTH = 2

GRID_W = 64
CTX_LEN = 256
N_MOD = 9
D_FF = 5632
EPS = 1e-6
ROPE_BASE = 10000.0
BLOCK = 128

A_HEADS = 8
A_KV_HEADS = 2
A_REP = A_HEADS // A_KV_HEADS
A_HEAD_DIM = 128
A_WINDOW = 128

B_HEADS = 8
B_NOPE = 128
B_ROPE = 64
B_QK = B_NOPE + B_ROPE
B_V = 128
B_Q_RANK = 768
B_KV_RANK = 256

A_Q_W = A_HEADS * A_HEAD_DIM
A_KV_W = A_KV_HEADS * A_HEAD_DIM
IN_SPLITS = (A_Q_W,
             A_Q_W + A_KV_W,
             A_Q_W + 2 * A_KV_W,
             A_Q_W + 2 * A_KV_W + B_Q_RANK,
             A_Q_W + 2 * A_KV_W + B_Q_RANK + B_KV_RANK)
IN_W = IN_SPLITS[-1] + B_ROPE
MIX_W = A_HEADS * A_HEAD_DIM + B_HEADS * B_V

POOL_WINDOWS = (2, 4, 8, 16)
POOL_GROUPS = len(POOL_WINDOWS)
POOL_GW = D_MODEL // POOL_GROUPS

N_ATTN_LAYERS = (DEPTH + 1) // 2
N_POOL_LAYERS = DEPTH // 2

kernel_name = 'hybrid_swa_mla_pool_macaron_dit'


def rmsnorm(x, g):
    xf = x.astype(jnp.float32)
    y = xf * lax.rsqrt(jnp.mean(xf * xf, axis=-1, keepdims=True) + EPS)
    return (y * g.astype(jnp.float32)).astype(x.dtype)


def modulate(h, shift, scale):
    return h * (1.0 + scale) + shift


def swiglu(h, w_gate, w_up, w_down):
    return (jax.nn.silu(h @ w_gate) * (h @ w_up)) @ w_down


def axial_rope(n, rot_dim, dtype):
    rows = n // GRID_W
    row = jnp.repeat(jnp.arange(rows), GRID_W).astype(jnp.float32)
    col = jnp.tile(jnp.arange(GRID_W), rows).astype(jnp.float32)
    nf = rot_dim // 4
    inv = ROPE_BASE ** (-jnp.arange(nf, dtype=jnp.float32) / nf)
    ang = jnp.concatenate([row[:, None] * inv, col[:, None] * inv], axis=-1)
    return jnp.cos(ang)[:, None, :].astype(dtype), jnp.sin(ang)[:, None, :].astype(dtype)


def apply_rope(x, cos, sin):
    half = x.shape[-1] // 2
    x1, x2 = x[..., :half], x[..., half:]
    return jnp.concatenate([x1 * cos - x2 * sin, x1 * sin + x2 * cos], axis=-1)


def window_gqa(q, k, v, kc, vc, sink):
    b, n = q.shape[:2]
    nc = kc.shape[1]
    nb = n // BLOCK
    scale = A_HEAD_DIM ** -0.5
    qb = q.reshape(b, nb, BLOCK, A_KV_HEADS, A_REP, A_HEAD_DIM)

    def band(t):
        tp = jnp.pad(t, ((0, 0), (BLOCK, BLOCK), (0, 0), (0, 0)))
        tp = tp.reshape(b, nb + 2, BLOCK, A_KV_HEADS, A_HEAD_DIM)
        return jnp.concatenate([tp[:, :-2], tp[:, 1:-1], tp[:, 2:]], axis=2)

    kb, vb = band(k), band(v)
    s_loc = jnp.einsum('bnqgrd,bnkgd->bngrqk', qb, kb).astype(jnp.float32) * scale
    blk = jnp.arange(nb)[:, None, None]
    qpos = blk * BLOCK + jnp.arange(BLOCK)[None, :, None]
    kpos = (blk - 1) * BLOCK + jnp.arange(3 * BLOCK)[None, None, :]
    valid = (jnp.abs(kpos - qpos) <= A_WINDOW) & (kpos >= 0) & (kpos < n)
    s_loc = jnp.where(valid[None, :, None, None], s_loc, -jnp.inf)
    s_ctx = jnp.einsum('bnqgrd,bcgd->bngrqc', qb, kc).astype(jnp.float32) * scale
    s_sink = jnp.broadcast_to(sink.astype(jnp.float32).reshape(1, 1, A_KV_HEADS, A_REP, 1, 1),
                              s_loc.shape[:-1] + (1,))
    p = jax.nn.softmax(jnp.concatenate([s_loc, s_ctx, s_sink], axis=-1), axis=-1).astype(v.dtype)
    kl = 3 * BLOCK
    o = (jnp.einsum('bngrqk,bnkgd->bnqgrd', p[..., :kl], vb)
         + jnp.einsum('bngrqc,bcgd->bnqgrd', p[..., kl:kl + nc], vc))
    return o.reshape(b, n, A_HEADS * A_HEAD_DIM)


def context_gqa(qc, kc, vc, sink):
    b, nc = qc.shape[:2]
    scale = A_HEAD_DIM ** -0.5
    qg = qc.reshape(b, nc, A_KV_HEADS, A_REP, A_HEAD_DIM)
    s = jnp.einsum('bqgrd,bkgd->bgrqk', qg, kc).astype(jnp.float32) * scale
    s_sink = jnp.broadcast_to(sink.astype(jnp.float32).reshape(1, A_KV_HEADS, A_REP, 1, 1),
                              s.shape[:-1] + (1,))
    p = jax.nn.softmax(jnp.concatenate([s, s_sink], axis=-1), axis=-1)[..., :nc].astype(vc.dtype)
    return jnp.einsum('bgrqk,bkgd->bqgrd', p, vc).reshape(b, nc, A_HEADS * A_HEAD_DIM)


def mla_keys(ckv, kr, kv_norm, w_ukv, rope):
    b, n = ckv.shape[:2]
    kv = (rmsnorm(ckv, kv_norm) @ w_ukv).reshape(b, n, B_HEADS, B_NOPE + B_V)
    kr = kr[:, :, None, :]
    if rope is not None:
        kr = apply_rope(kr, *rope)
    k = jnp.concatenate([kv[..., :B_NOPE], jnp.broadcast_to(kr, (b, n, B_HEADS, B_ROPE))], axis=-1)
    return k, kv[..., B_NOPE:]


def mla_queries(cq, q_norm, w_uq, rope):
    b, n = cq.shape[:2]
    q = (rmsnorm(cq, q_norm) @ w_uq).reshape(b, n, B_HEADS, B_QK)
    if rope is None:
        return q
    return jnp.concatenate([q[..., :B_NOPE], apply_rope(q[..., B_NOPE:], *rope)], axis=-1)


def block_dense_attention(q, k, v):
    b, n = q.shape[:2]
    nb = n // BLOCK
    scale = B_QK ** -0.5
    qb = jnp.moveaxis(q.reshape(b, nb, BLOCK, B_HEADS, B_QK), 1, 0)

    def one_block(qblk):
        s = jnp.einsum('bqhd,bkhd->bhqk', qblk, k).astype(jnp.float32) * scale
        p = jax.nn.softmax(s, axis=-1).astype(v.dtype)
        return jnp.einsum('bhqk,bkhd->bqhd', p, v)

    o = lax.map(one_block, qb)
    return jnp.moveaxis(o, 0, 1).reshape(b, n, B_HEADS * B_V)


def attention_mixer(h, hc, ctx_out, w_in, sink, q_norm, w_uq, kv_norm, w_ukv, w_out, rope_a, rope_b):
    b, n, _ = h.shape
    nc = hc.shape[1]
    aq, ak, av, bcq, bckv, bkr = jnp.split(h @ w_in, IN_SPLITS, axis=-1)
    caq, cak, cav, cbcq, cbckv, cbkr = jnp.split(hc @ w_in, IN_SPLITS, axis=-1)
    q_a = apply_rope(aq.reshape(b, n, A_HEADS, A_HEAD_DIM), *rope_a)
    k_a = apply_rope(ak.reshape(b, n, A_KV_HEADS, A_HEAD_DIM), *rope_a)
    v_a = av.reshape(b, n, A_KV_HEADS, A_HEAD_DIM)
    kc_a = cak.reshape(b, nc, A_KV_HEADS, A_HEAD_DIM)
    vc_a = cav.reshape(b, nc, A_KV_HEADS, A_HEAD_DIM)
    o_a = window_gqa(q_a, k_a, v_a, kc_a, vc_a, sink)
    k_b, v_b = mla_keys(bckv, bkr, kv_norm, w_ukv, rope_b)
    kc_b, vc_b = mla_keys(cbckv, cbkr, kv_norm, w_ukv, None)
    q_b = mla_queries(bcq, q_norm, w_uq, rope_b)
    o_b = block_dense_attention(q_b, jnp.concatenate([kc_b, k_b], axis=1),
                                jnp.concatenate([vc_b, v_b], axis=1))
    y = jnp.concatenate([o_a, o_b], axis=-1) @ w_out
    if not ctx_out:
        return y, None
    oc_a = context_gqa(caq.reshape(b, nc, A_HEADS, A_HEAD_DIM), kc_a, vc_a, sink)
    oc_b = block_dense_attention(mla_queries(cbcq, q_norm, w_uq, None), kc_b, vc_b)
    yc = jnp.concatenate([oc_a, oc_b], axis=-1) @ w_out
    return y, yc


def centred_window_mean(xg, w):
    n = xg.shape[1]
    cs = jnp.pad(jnp.cumsum(xg.astype(jnp.float32), axis=1), ((0, 0), (1, 0), (0, 0)))
    t = jnp.arange(n)
    lo = jnp.clip(t - w // 2, 0, n)
    hi = jnp.clip(t - w // 2 + w, 0, n)
    s = jnp.take(cs, hi, axis=1) - jnp.take(cs, lo, axis=1)
    return (s / (hi - lo).astype(jnp.float32)[None, :, None]).astype(xg.dtype)


def pool_mixer(h, w, scale):
    b, n, d = h.shape
    hg = h.reshape(b, n, POOL_GROUPS, POOL_GW)
    pooled = jnp.stack([centred_window_mean(hg[:, :, g], POOL_WINDOWS[g]) for g in range(POOL_GROUPS)],
                       axis=2)
    y = jnp.einsum('bngc,gcd->bngd', pooled - hg, w).reshape(b, n, d)
    return y * scale


def setup_inputs(seed: int = 0) -> dict:
    key = jax.random.key(seed)
    ks = jax.random.split(key, 32)
    cnt = [0]
    f32 = jnp.float32

    def nk():
        cnt[0] += 1
        return ks[cnt[0] - 1]

    def dense(shape, fan_in, gain=1.0):
        return jax.random.normal(nk(), shape, f32) * (gain * fan_in ** -0.5)

    def gainv(shape):
        return 1.0 + 0.05 * jax.random.normal(nk(), shape, f32)

    def small(shape, s):
        return s * jax.random.normal(nk(), shape, f32)

    D, F = D_MODEL, D_FF
    NA, NP = N_ATTN_LAYERS, N_POOL_LAYERS
    return {
        'x': jax.random.normal(nk(), (BATCH, SEQ, D), f32),
        'c': jax.random.normal(nk(), (BATCH, D), f32),
        'ctx': jax.random.normal(nk(), (BATCH, CTX_LEN, D), f32),
        'c_ctx': jax.random.normal(nk(), (D,), f32),
        'w_ada': dense((DEPTH, D, N_MOD * D), D, 0.5),
        'b_ada': small((DEPTH, N_MOD * D), 0.02),
        'norm_ffn1': gainv((DEPTH, D)),
        'norm_mix': gainv((DEPTH, D)),
        'norm_ffn2': gainv((DEPTH, D)),
        'ffn1_w_gate': dense((DEPTH, D, F), D),
        'ffn1_w_up': dense((DEPTH, D, F), D),
        'ffn1_w_down': dense((DEPTH, F, D), F),
        'ffn2_w_gate': dense((DEPTH, D, F), D),
        'ffn2_w_up': dense((DEPTH, D, F), D),
        'ffn2_w_down': dense((DEPTH, F, D), F),
        'attn_w_in': dense((NA, D, IN_W), D),
        'attn_sink': small((NA, A_HEADS), 0.5),
        'mla_q_norm': gainv((NA, B_Q_RANK)),
        'mla_w_uq': dense((NA, B_Q_RANK, B_HEADS * B_QK), B_Q_RANK),
        'mla_kv_norm': gainv((NA, B_KV_RANK)),
        'mla_w_ukv': dense((NA, B_KV_RANK, B_HEADS * (B_NOPE + B_V)), B_KV_RANK),
        'attn_w_out': dense((NA, MIX_W, D), MIX_W),
        'pool_w': dense((NP, POOL_GROUPS, POOL_GW, POOL_GW), POOL_GW),
        'pool_scale': gainv((NP, D)),
        'final_norm': gainv((D,)),
    }


def reference(x, c, ctx, c_ctx, w_ada, b_ada, norm_ffn1, norm_mix, norm_ffn2,
              ffn1_w_gate, ffn1_w_up, ffn1_w_down, ffn2_w_gate, ffn2_w_up, ffn2_w_down,
              attn_w_in, attn_sink, mla_q_norm, mla_w_uq, mla_kv_norm, mla_w_ukv, attn_w_out,
              pool_w, pool_scale, final_norm):
    n = x.shape[1]
    rope_a = axial_rope(n, A_HEAD_DIM, x.dtype)
    rope_b = axial_rope(n, B_ROPE, x.dtype)
    cx = ctx
    for l in range(DEPTH):
        is_attn = (l % 2 == 0)
        ctx_out = any(j % 2 == 0 for j in range(l + 1, DEPTH))
        ctx_in = is_attn or ctx_out
        m = jnp.split((jax.nn.silu(c) @ w_ada[l] + b_ada[l])[:, None, :], N_MOD, axis=-1)
        mc = jnp.split(jax.nn.silu(c_ctx) @ w_ada[l] + b_ada[l], N_MOD, axis=-1)
        x = x + 0.5 * m[2] * swiglu(modulate(rmsnorm(x, norm_ffn1[l]), m[0], m[1]),
                                    ffn1_w_gate[l], ffn1_w_up[l], ffn1_w_down[l])
        if ctx_in:
            cx = cx + 0.5 * mc[2] * swiglu(modulate(rmsnorm(cx, norm_ffn1[l]), mc[0], mc[1]),
                                           ffn1_w_gate[l], ffn1_w_up[l], ffn1_w_down[l])
        h = modulate(rmsnorm(x, norm_mix[l]), m[3], m[4])
        hc = modulate(rmsnorm(cx, norm_mix[l]), mc[3], mc[4]) if ctx_in else None
        i = l // 2
        if is_attn:
            y, yc = attention_mixer(h, hc, ctx_out, attn_w_in[i], attn_sink[i], mla_q_norm[i], mla_w_uq[i],
                                    mla_kv_norm[i], mla_w_ukv[i], attn_w_out[i], rope_a, rope_b)
        else:
            y = pool_mixer(h, pool_w[i], pool_scale[i])
            yc = pool_mixer(hc, pool_w[i], pool_scale[i]) if ctx_out else None
        x = x + m[5] * y
        if ctx_out:
            cx = cx + mc[5] * yc
        x = x + 0.5 * m[8] * swiglu(modulate(rmsnorm(x, norm_ffn2[l]), m[6], m[7]),
                                    ffn2_w_gate[l], ffn2_w_up[l], ffn2_w_down[l])
        if ctx_out:
            cx = cx + 0.5 * mc[8] * swiglu(modulate(rmsnorm(cx, norm_ffn2[l]), mc[6], mc[7]),
                                           ffn2_w_gate[l], ffn2_w_up[l], ffn2_w_down[l])
    return rmsnorm(x, final_norm)
```

```python
import functools

import jax
import jax.numpy as jnp
from jax import lax
from jax.experimental import pallas as pl
from jax.experimental.pallas import tpu as pltpu

F32 = jnp.float32
BF16 = jnp.bfloat16

LANES = 128
SUBLANES = 8
MIB = 1 << 20

EPS = 1e-6
ROPE_BASE = 10000.0
GRID_W = 64
N_MOD = 9
A_HEADS = 8
A_KV_HEADS = 2
A_REP = A_HEADS // A_KV_HEADS
A_HEAD_DIM = 128
A_WINDOW = 128
B_HEADS = 8
B_NOPE = 128
B_ROPE = 64
B_QK = B_NOPE + B_ROPE
B_V = 128
B_Q_RANK = 768
B_KV_RANK = 256
B_PAD = 2 * LANES
A_Q_W = A_HEADS * A_HEAD_DIM
A_KV_W = A_KV_HEADS * A_HEAD_DIM
OFF_AK = A_Q_W
OFF_AV = OFF_AK + A_KV_W
OFF_CQ = OFF_AV + A_KV_W
OFF_CKV = OFF_CQ + B_Q_RANK
OFF_KR = OFF_CKV + B_KV_RANK
IN_W = OFF_KR + B_ROPE
IN_W_PAD = OFF_KR + LANES
POOL_WINDOWS = (2, 4, 8, 16)
POOL_HALO = SUBLANES
NEG_BIG = -1e30

MOD_ROWS = 8


def _params(semantics, vmem_mib):
    return pltpu.CompilerParams(dimension_semantics=semantics, vmem_limit_bytes=vmem_mib * MIB)


def _const_spec(shape):
    nd = len(shape)
    return pl.BlockSpec(shape, lambda *_: (0,) * nd, pipeline_mode=pl.Buffered(1))


def _rms_mod(x, g, shift, scale):
    y = x * lax.rsqrt(jnp.mean(x * x, axis=-1, keepdims=True) + EPS)
    return (y * g) * (1.0 + scale) + shift


def _dot(a, b):
    return jnp.dot(a, b, preferred_element_type=F32)


def _dot_nt(a, b):
    return lax.dot_general(a, b, (((1,), (1,)), ((), ())), preferred_element_type=F32)


def _mods_kernel(c_ref, w_ref, b_ref, o_ref):
    c = c_ref[...]
    a = (c * jax.nn.sigmoid(c)).astype(BF16)
    o_ref[0] = _dot(a, w_ref[0].astype(BF16)) + b_ref[0]


def _mods(c, c_ctx, w_ada, b_ada):
    depth, d, nd = w_ada.shape
    b = c.shape[0]
    assert b + 1 <= MOD_ROWS
    cp = jnp.concatenate([c, c_ctx[None, :], jnp.zeros((MOD_ROWS - b - 1, d), F32)], axis=0)
    tn = min(d, 1024)
    assert nd % tn == 0
    out = pl.pallas_call(
        _mods_kernel,
        grid=(depth, nd // tn),
        in_specs=[
            pl.BlockSpec((MOD_ROWS, d), lambda l, j: (0, 0)),
            pl.BlockSpec((1, d, tn), lambda l, j: (l, 0, j)),
            pl.BlockSpec((1, 1, tn), lambda l, j: (l, 0, j)),
        ],
        out_specs=pl.BlockSpec((1, MOD_ROWS, tn), lambda l, j: (l, 0, j)),
        out_shape=jax.ShapeDtypeStruct((depth, MOD_ROWS, nd), F32),
        compiler_params=_params(("arbitrary", "arbitrary"), 40),
        name="adaln_mods",
    )(cp, w_ada, b_ada.reshape(depth, 1, nd))
    return out.reshape(depth, MOD_ROWS * N_MOD, 1, d)


def _mod_spec(d, row_of_tile, k):
    return pl.BlockSpec((1, 1, d), lambda i, *_: (row_of_tile(i) * N_MOD + k, 0, 0))


def _ffn_kernel(x_ref, g_ref, sh_ref, sc_ref, gt_ref, wg_ref, wu_ref, wd_ref, fn_ref,
                o_ref, h_ref, acc_ref, *, final_norm):
    j = pl.program_id(1)

    @pl.when(j == 0)
    def _():
        h_ref[...] = _rms_mod(x_ref[...], g_ref[...], sh_ref[0], sc_ref[0]).astype(BF16)
        acc_ref[...] = jnp.zeros_like(acc_ref)

    h = h_ref[...]
    gate = _dot(h, wg_ref[...])
    up = _dot(h, wu_ref[...])
    a = (gate * jax.nn.sigmoid(gate)) * up
    acc_ref[...] += _dot(a.astype(BF16), wd_ref[...])

    @pl.when(j == pl.num_programs(1) - 1)
    def _():
        out = x_ref[...] + (0.5 * gt_ref[0]) * acc_ref[...]
        if final_norm:
            out = out * lax.rsqrt(jnp.mean(out * out, axis=-1, keepdims=True) + EPS) * fn_ref[...]
        o_ref[...] = out


def _ffn(x, mods, mod_base, row_of_tile, g, wg, wu, wd, fn, *, tm, tf, final_norm):
    t, d = x.shape
    f = wg.shape[1]
    return pl.pallas_call(
        functools.partial(_ffn_kernel, final_norm=final_norm),
        grid=(t // tm, f // tf),
        in_specs=[
            pl.BlockSpec((tm, d), lambda i, j: (i, 0)),
            pl.BlockSpec((1, d), lambda i, j: (0, 0)),
            _mod_spec(d, row_of_tile, mod_base),
            _mod_spec(d, row_of_tile, mod_base + 1),
            _mod_spec(d, row_of_tile, mod_base + 2),
            pl.BlockSpec((d, tf), lambda i, j: (0, j)),
            pl.BlockSpec((d, tf), lambda i, j: (0, j)),
            pl.BlockSpec((tf, d), lambda i, j: (j, 0)),
            pl.BlockSpec((1, d), lambda i, j: (0, 0)),
        ],
        out_specs=pl.BlockSpec((tm, d), lambda i, j: (i, 0)),
        out_shape=jax.ShapeDtypeStruct((t, d), F32),
        scratch_shapes=[pltpu.VMEM((tm, d), BF16), pltpu.VMEM((tm, d), F32)],
        compiler_params=_params(("parallel", "arbitrary"), 56),
        name="swiglu_half_step",
    )(x, g, mods, mods, mods, wg, wu, wd, fn)


def _rope_a(x, cos, sin):
    return x * cos + pltpu.roll(x, A_HEAD_DIM // 2, 1) * sin


def _rope_b(x, cos, sin_lo, sin_hi):
    half = B_ROPE // 2
    return x * cos + pltpu.roll(x, LANES - half, 1) * sin_lo + pltpu.roll(x, half, 1) * sin_hi


def _inproj_kernel(x_ref, g_ref, sh_ref, sc_ref, win_ref, qn_ref, wuq_ref, kvn_ref, wukv_ref,
                   cosa_ref, sina_ref, cosb_ref, sinlo_ref, sinhi_ref,
                   qa_ref, ka_ref, va_ref, qb_ref, kb_ref, vb_ref):
    h = _rms_mod(x_ref[...], g_ref[...], sh_ref[0], sc_ref[0]).astype(BF16)
    cosa, sina = cosa_ref[...], sina_ref[...]
    cosb, sinlo, sinhi = cosb_ref[...], sinlo_ref[...], sinhi_ref[...]

    aq = _dot(h, win_ref[:, 0:OFF_AK])
    for s in range(A_HEADS):
        sl = slice(s * LANES, (s + 1) * LANES)
        qa_ref[:, sl] = _rope_a(aq[:, sl], cosa, sina).astype(BF16)
    akv = _dot(h, win_ref[:, OFF_AK:OFF_CQ])
    for s in range(A_KV_HEADS):
        sl = slice(s * LANES, (s + 1) * LANES)
        ka_ref[:, sl] = _rope_a(akv[:, sl], cosa, sina).astype(BF16)
    va_ref[...] = akv[:, A_KV_W:].astype(BF16)

    cq = _dot(h, win_ref[:, OFF_CQ:OFF_CKV])
    cqn = (cq * lax.rsqrt(jnp.mean(cq * cq, axis=-1, keepdims=True) + EPS) * qn_ref[...]).astype(BF16)
    qb = _dot(cqn, wuq_ref[...]) * (B_QK ** -0.5)
    for hd in range(B_HEADS):
        lo = hd * B_PAD
        qb_ref[:, lo:lo + LANES] = qb[:, lo:lo + LANES].astype(BF16)
        qb_ref[:, lo + LANES:lo + B_PAD] = _rope_b(qb[:, lo + LANES:lo + B_PAD], cosb, sinlo, sinhi).astype(BF16)

    ckr = _dot(h, win_ref[:, OFF_CKV:IN_W_PAD])
    ckv = ckr[:, 0:B_KV_RANK]
    ckvn = (ckv * lax.rsqrt(jnp.mean(ckv * ckv, axis=-1, keepdims=True) + EPS) * kvn_ref[...]).astype(BF16)
    kr = _rope_b(ckr[:, B_KV_RANK:], cosb, sinlo, sinhi).astype(BF16)
    kv = _dot(ckvn, wukv_ref[...])
    for hd in range(B_HEADS):
        lo = hd * B_PAD
        kb_ref[:, lo:lo + LANES] = kv[:, hd * B_NOPE:(hd + 1) * B_NOPE].astype(BF16)
        kb_ref[:, lo + LANES:lo + B_PAD] = kr
    vb_ref[...] = kv[:, B_HEADS * B_NOPE:].astype(BF16)


def _inproj(x, mods, row_of_tile, g, win, qn, wuq, kvn, wukv, tables, table_tile, *, tm):
    t, d = x.shape
    tok = lambda w: pl.BlockSpec((tm, w), lambda i: (i, 0))
    tab = pl.BlockSpec((tm, LANES), lambda i: (table_tile(i), 0))
    widths = (A_Q_W, A_KV_W, A_KV_W, B_HEADS * B_PAD, B_HEADS * B_PAD, B_HEADS * B_V)
    return pl.pallas_call(
        _inproj_kernel,
        grid=(t // tm,),
        in_specs=[
            tok(d),
            _const_spec((1, d)),
            _mod_spec(d, row_of_tile, 3),
            _mod_spec(d, row_of_tile, 4),
            _const_spec(win.shape),
            _const_spec(qn.shape),
            _const_spec(wuq.shape),
            _const_spec(kvn.shape),
            _const_spec(wukv.shape),
            tab, tab, tab, tab, tab,
        ],
        out_specs=[tok(w) for w in widths],
        out_shape=[jax.ShapeDtypeStruct((t, w), BF16) for w in widths],
        compiler_params=_params(("parallel",), 56),
        name="attn_in_proj",
    )(x, g, mods, mods, win, qn, wuq, kvn, wukv, *tables)


def _window_kernel(sink_ref, q_ref, k_ref, v_ref, kc_ref, vc_ref, o_ref, *, tq, n):
    qi = pl.program_id(1)
    span = tq + 2 * A_WINDOW
    start = pl.multiple_of(jnp.clip(qi * tq - A_WINDOW, 0, n - span), A_WINDOW)
    qpos = qi * tq + lax.broadcasted_iota(jnp.int32, (tq, span), 0)
    kpos = start + lax.broadcasted_iota(jnp.int32, (tq, span), 1)
    valid = jnp.abs(kpos - qpos) <= A_WINDOW
    scale = A_HEAD_DIM ** -0.5
    for g in range(A_KV_HEADS):
        gl = slice(g * A_HEAD_DIM, (g + 1) * A_HEAD_DIM)
        k = k_ref[0, pl.ds(start, span), gl]
        v = v_ref[0, pl.ds(start, span), gl]
        kc = kc_ref[0, :, gl]
        vc = vc_ref[0, :, gl]
        for r in range(A_REP):
            hd = g * A_REP + r
            hl = slice(hd * A_HEAD_DIM, (hd + 1) * A_HEAD_DIM)
            q = q_ref[0, :, hl]
            s_loc = jnp.where(valid, _dot_nt(q, k) * scale, NEG_BIG)
            s_ctx = _dot_nt(q, kc) * scale
            sink = sink_ref[hd]
            m = jnp.maximum(jnp.maximum(jnp.max(s_loc, axis=-1, keepdims=True),
                                        jnp.max(s_ctx, axis=-1, keepdims=True)), sink)
            p_loc = jnp.exp(s_loc - m)
            p_ctx = jnp.exp(s_ctx - m)
            denom = (jnp.sum(p_loc, axis=-1, keepdims=True) + jnp.sum(p_ctx, axis=-1, keepdims=True)
                     + jnp.exp(sink - m))
            o = _dot(p_loc.astype(BF16), v) + _dot(p_ctx.astype(BF16), vc)
            o_ref[0, :, hl] = (o / denom).astype(BF16)


def _window_gqa(qa, ka, va, kca, vca, sink, *, tq):
    b, n, _ = qa.shape
    nc = kca.shape[1]
    assert n >= tq + 2 * A_WINDOW
    return pl.pallas_call(
        functools.partial(_window_kernel, tq=tq, n=n),
        grid=(b, n // tq),
        in_specs=[
            pl.BlockSpec(memory_space=pltpu.SMEM),
            pl.BlockSpec((1, tq, A_Q_W), lambda bi, qi: (bi, qi, 0)),
            pl.BlockSpec((1, n, A_KV_W), lambda bi, qi: (bi, 0, 0)),
            pl.BlockSpec((1, n, A_KV_W), lambda bi, qi: (bi, 0, 0)),
            pl.BlockSpec((1, nc, A_KV_W), lambda bi, qi: (bi, 0, 0)),
            pl.BlockSpec((1, nc, A_KV_W), lambda bi, qi: (bi, 0, 0)),
        ],
        out_specs=pl.BlockSpec((1, tq, A_Q_W), lambda bi, qi: (bi, qi, 0)),
        out_shape=jax.ShapeDtypeStruct((b, n, A_Q_W), BF16),
        compiler_params=_params(("parallel", "arbitrary"), 48),
        name="window_gqa",
    )(sink, qa, ka, va, kca, vca)


def _mla_kernel(q_ref, k_ref, v_ref, kc_ref, vc_ref, o_ref, m_ref, l_ref, acc_ref):
    ki = pl.program_id(3)

    @pl.when(ki == 0)
    def _():
        s = _dot_nt(q_ref[0], kc_ref[0])
        m = jnp.max(s, axis=-1, keepdims=True)
        p = jnp.exp(s - m)
        m_ref[...] = m
        l_ref[...] = jnp.sum(p, axis=-1, keepdims=True)
        acc_ref[...] = _dot(p.astype(BF16), vc_ref[0])

    @pl.when(ki > 0)
    def _():
        s = _dot_nt(q_ref[0], k_ref[0])
        m_old = m_ref[...]
        m = jnp.maximum(m_old, jnp.max(s, axis=-1, keepdims=True))
        alpha = jnp.exp(m_old - m)
        p = jnp.exp(s - m)
        m_ref[...] = m
        l_ref[...] = alpha * l_ref[...] + jnp.sum(p, axis=-1, keepdims=True)
        acc_ref[...] = alpha * acc_ref[...] + _dot(p.astype(BF16), v_ref[0])

    @pl.when(ki == pl.num_programs(3) - 1)
    def _():
        o_ref[0] = (acc_ref[...] / l_ref[...]).astype(BF16)


def _mla_attention(qb, kb, vb, kcb, vcb, *, tq, tk):
    b, n, _ = qb.shape
    nc = kcb.shape[1]
    kv_blk = lambda bi, hi, qi, ki: (bi, jnp.maximum(ki - 1, 0), hi)
    return pl.pallas_call(
        _mla_kernel,
        grid=(b, B_HEADS, n // tq, 1 + n // tk),
        in_specs=[
            pl.BlockSpec((1, tq, B_PAD), lambda bi, hi, qi, ki: (bi, qi, hi)),
            pl.BlockSpec((1, tk, B_PAD), kv_blk),
            pl.BlockSpec((1, tk, B_V), kv_blk),
            pl.BlockSpec((1, nc, B_PAD), lambda bi, hi, qi, ki: (bi, 0, hi)),
            pl.BlockSpec((1, nc, B_V), lambda bi, hi, qi, ki: (bi, 0, hi)),
        ],
        out_specs=pl.BlockSpec((1, tq, B_V), lambda bi, hi, qi, ki: (bi, qi, hi)),
        out_shape=jax.ShapeDtypeStruct((b, n, B_HEADS * B_V), BF16),
        scratch_shapes=[pltpu.VMEM((tq, 1), F32), pltpu.VMEM((tq, 1), F32), pltpu.VMEM((tq, B_V), F32)],
        compiler_params=_params(("parallel", "parallel", "parallel", "arbitrary"), 48),
        name="mla_attention",
    )(qb, kb, vb, kcb, vcb)


def _outproj_kernel(x_ref, oa_ref, ob_ref, w_ref, gt_ref, o_ref):
    ka = oa_ref.shape[1]
    y = _dot(oa_ref[...], w_ref[0:ka, :]) + _dot(ob_ref[...], w_ref[ka:, :])
    o_ref[...] = x_ref[...] + gt_ref[0] * y


def _outproj(x, oa, ob, w, mods, row_of_tile, *, tm):
    t, d = x.shape
    return pl.pallas_call(
        _outproj_kernel,
        grid=(t // tm,),
        in_specs=[
            pl.BlockSpec((tm, d), lambda i: (i, 0)),
            pl.BlockSpec((tm, oa.shape[1]), lambda i: (i, 0)),
            pl.BlockSpec((tm, ob.shape[1]), lambda i: (i, 0)),
            _const_spec(w.shape),
            _mod_spec(d, row_of_tile, 5),
        ],
        out_specs=pl.BlockSpec((tm, d), lambda i: (i, 0)),
        out_shape=jax.ShapeDtypeStruct((t, d), F32),
        compiler_params=_params(("parallel",), 48),
        name="attn_out_proj",
    )(x, oa, ob, w, mods)


def _pool_kernel(x_ref, xp_ref, xn_ref, g_ref, sh_ref, sc_ref, gt_ref, w_ref, ps_ref, o_ref, hp_ref,
                 *, tm, n):
    i = pl.program_id(1)
    last = pl.num_programs(1) - 1
    g, sh, sc = g_ref[...], sh_ref[0], sc_ref[0]
    x = x_ref[0]
    hp_ref[0:POOL_HALO, :] = jnp.where(i > 0, _rms_mod(xp_ref[0], g, sh, sc), 0.0)
    hp_ref[POOL_HALO:POOL_HALO + tm, :] = _rms_mod(x, g, sh, sc)
    hp_ref[POOL_HALO + tm:, :] = jnp.where(i < last, _rms_mod(xn_ref[0], g, sh, sc), 0.0)

    t = i * tm + lax.broadcasted_iota(jnp.int32, (tm, 1), 0)
    gw = x.shape[1] // len(POOL_WINDOWS)
    for gi, w in enumerate(POOL_WINDOWS):
        cols = slice(gi * gw, (gi + 1) * gw)
        tot = hp_ref[pl.ds(POOL_HALO - w // 2, tm), cols]
        for k in range(1, w):
            tot = tot + hp_ref[pl.ds(POOL_HALO - w // 2 + k, tm), cols]
        cnt = (jnp.minimum(t - w // 2 + w, n) - jnp.maximum(t - w // 2, 0)).astype(F32)
        diff = tot / cnt - hp_ref[pl.ds(POOL_HALO, tm), cols]
        y = _dot(diff.astype(BF16), w_ref[gi]) * ps_ref[:, cols]
        o_ref[0, :, cols] = x[:, cols] + gt_ref[0][:, cols] * y


def _pool_mixer(x3, mods, g, w, ps, *, tm):
    b, n, d = x3.shape
    hb = tm // POOL_HALO
    mod = lambda k: pl.BlockSpec((1, 1, d), lambda bi, i: (bi * N_MOD + k, 0, 0))
    return pl.pallas_call(
        functools.partial(_pool_kernel, tm=tm, n=n),
        grid=(b, n // tm),
        in_specs=[
            pl.BlockSpec((1, tm, d), lambda bi, i: (bi, i, 0)),
            pl.BlockSpec((1, POOL_HALO, d), lambda bi, i: (bi, jnp.maximum(i * hb - 1, 0), 0)),
            pl.BlockSpec((1, POOL_HALO, d), lambda bi, i: (bi, jnp.minimum((i + 1) * hb, n // POOL_HALO - 1), 0)),
            pl.BlockSpec((1, d), lambda bi, i: (0, 0)),
            mod(3), mod(4), mod(5),
            pl.BlockSpec(w.shape, lambda bi, i: (0, 0, 0)),
            pl.BlockSpec((1, d), lambda bi, i: (0, 0)),
        ],
        out_specs=pl.BlockSpec((1, tm, d), lambda bi, i: (bi, i, 0)),
        out_shape=jax.ShapeDtypeStruct((b, n, d), F32),
        scratch_shapes=[pltpu.VMEM((tm + 2 * POOL_HALO, d), F32)],
        compiler_params=_params(("parallel", "arbitrary"), 48),
        name="pool_mixer",
    )(x3, x3, x3, g, mods, mods, mods, w, ps)


def _rope_tables(n):
    rows = n // GRID_W
    row = jnp.repeat(jnp.arange(rows), GRID_W).astype(F32)
    col = jnp.tile(jnp.arange(GRID_W), rows).astype(F32)

    def angles(rot_dim):
        nf = rot_dim // 4
        inv = ROPE_BASE ** (-jnp.arange(nf, dtype=F32) / nf)
        return jnp.concatenate([row[:, None] * inv, col[:, None] * inv], axis=-1)

    ang_a, ang_b = angles(A_HEAD_DIM), angles(B_ROPE)
    ca, sa = jnp.cos(ang_a), jnp.sin(ang_a)
    cb, sb = jnp.cos(ang_b), jnp.sin(ang_b)
    zb = jnp.zeros_like(cb)
    zpad = jnp.zeros((n, LANES - B_ROPE), F32)
    return (jnp.concatenate([ca, ca], axis=-1), jnp.concatenate([-sa, sa], axis=-1),
            jnp.concatenate([cb, cb, zpad], axis=-1), jnp.concatenate([-sb, zb, zpad], axis=-1),
            jnp.concatenate([zb, sb, zpad], axis=-1))


def _identity_tables(n):
    one = jnp.ones((n, LANES), F32)
    zero = jnp.zeros((n, LANES), F32)
    return (one, zero, one, zero, zero)


def _attn_weights(w_in, w_uq, w_ukv):
    d = w_in.shape[0]
    win = jnp.concatenate([w_in, jnp.zeros((d, IN_W_PAD - IN_W), w_in.dtype)], axis=1).astype(BF16)
    uq = w_uq.reshape(B_Q_RANK, B_HEADS, B_QK)
    uq = jnp.concatenate([uq, jnp.zeros((B_Q_RANK, B_HEADS, B_PAD - B_QK), uq.dtype)], axis=-1)
    ukv = w_ukv.reshape(B_KV_RANK, B_HEADS, B_NOPE + B_V)
    ukv = jnp.concatenate([ukv[:, :, :B_NOPE].reshape(B_KV_RANK, -1), ukv[:, :, B_NOPE:].reshape(B_KV_RANK, -1)],
                          axis=1)
    return win, uq.reshape(B_Q_RANK, B_HEADS * B_PAD).astype(BF16), ukv.astype(BF16)


def kernel(x, c, ctx, c_ctx, w_ada, b_ada, norm_ffn1, norm_mix, norm_ffn2, ffn1_w_gate, ffn1_w_up, ffn1_w_down,
           ffn2_w_gate, ffn2_w_up, ffn2_w_down, attn_w_in, attn_sink, mla_q_norm, mla_w_uq, mla_kv_norm, mla_w_ukv,
           attn_w_out, pool_w, pool_scale, final_norm):
    b, n, d = x.shape
    nc = ctx.shape[1]
    depth = w_ada.shape[0]
    f = ffn1_w_gate.shape[2]
    t = b * n

    tm = min(512, n)
    tf = min(512, f)
    tmc = min(256, nc)
    tq_a = min(256, n)
    tq_b = min(1024, n)
    tk_b = min(1024, n)
    assert n % tm == 0 and f % tf == 0 and nc % tmc == 0 and n % GRID_W == 0

    tiles_per_sample = n // tm
    x_row = lambda i: i // tiles_per_sample
    ctx_row = lambda i: b

    mods = _mods(c, c_ctx, w_ada, b_ada)
    rope = _rope_tables(n)
    fn = final_norm.reshape(1, d)

    xs = x.reshape(t, d)
    cx = ctx.reshape(b * nc, d)
    for l in range(depth):
        is_attn = l % 2 == 0
        ctx_out = any(j % 2 == 0 for j in range(l + 1, depth))
        ctx_in = is_attn or ctx_out
        i = l // 2
        m = mods[l]
        g1, gm, g2 = (v[l].reshape(1, d) for v in (norm_ffn1, norm_mix, norm_ffn2))
        w1 = tuple(w[l].astype(BF16) for w in (ffn1_w_gate, ffn1_w_up, ffn1_w_down))
        w2 = tuple(w[l].astype(BF16) for w in (ffn2_w_gate, ffn2_w_up, ffn2_w_down))

        xs = _ffn(xs, m, 0, x_row, g1, *w1, fn, tm=tm, tf=tf, final_norm=False)
        if ctx_in:
            cx = _ffn(cx, m, 0, ctx_row, g1, *w1, fn, tm=tmc, tf=tf, final_norm=False)

        if is_attn:
            win, wuq, wukv = _attn_weights(attn_w_in[i], mla_w_uq[i], mla_w_ukv[i])
            qn = mla_q_norm[i].reshape(1, -1)
            kvn = mla_kv_norm[i].reshape(1, -1)
            qa, ka, va, qb, kb, vb = _inproj(xs, m, x_row, gm, win, qn, wuq, kvn, wukv, rope,
                                             lambda ti: ti % tiles_per_sample, tm=tm)
            _, kca, vca, _, kcb, vcb = _inproj(cx, m, ctx_row, gm, win, qn, wuq, kvn, wukv,
                                               _identity_tables(tmc), lambda ti: 0, tm=tmc)
            r3 = lambda a, rows: a.reshape(b, rows, a.shape[-1])
            oa = _window_gqa(r3(qa, n), r3(ka, n), r3(va, n), r3(kca, nc), r3(vca, nc), attn_sink[i], tq=tq_a)
            ob = _mla_attention(r3(qb, n), r3(kb, n), r3(vb, n), r3(kcb, nc), r3(vcb, nc), tq=tq_b, tk=tk_b)
            xs = _outproj(xs, oa.reshape(t, -1), ob.reshape(t, -1), attn_w_out[i].astype(BF16), m, x_row, tm=tm)
            if ctx_out:
                raise NotImplementedError("context-stream attention output is not needed at this depth")
        else:
            xs = _pool_mixer(xs.reshape(b, n, d), m, gm, pool_w[i].astype(BF16), pool_scale[i].reshape(1, d),
                             tm=tm).reshape(t, d)
            if ctx_out:
                raise NotImplementedError("context-stream pooling output is not needed at this depth")

        xs = _ffn(xs, m, 6, x_row, g2, *w2, fn, tm=tm, tf=tf, final_norm=(l == depth - 1))
        if ctx_out:
            cx = _ffn(cx, m, 6, ctx_row, g2, *w2, fn, tm=tmc, tf=tf, final_norm=False)
    return xs.reshape(b, n, d)
```

```python
import functools

import jax
import jax.numpy as jnp
from jax import lax
from jax.experimental import pallas as pl
from jax.experimental.pallas import tpu as pltpu

F32 = jnp.float32
BF16 = jnp.bfloat16

LANES = 128
SUBLANES = 8
MIB = 1 << 20

EPS = 1e-6
ROPE_BASE = 10000.0
GRID_W = 64
N_MOD = 9
A_HEADS = 8
A_KV_HEADS = 2
A_REP = A_HEADS // A_KV_HEADS
A_HEAD_DIM = 128
A_WINDOW = 128
B_HEADS = 8
B_NOPE = 128
B_ROPE = 64
B_QK = B_NOPE + B_ROPE
B_V = 128
B_Q_RANK = 768
B_KV_RANK = 256
B_PAD = 2 * LANES
A_Q_W = A_HEADS * A_HEAD_DIM
A_KV_W = A_KV_HEADS * A_HEAD_DIM
OFF_AK = A_Q_W
OFF_AV = OFF_AK + A_KV_W
OFF_CQ = OFF_AV + A_KV_W
OFF_CKV = OFF_CQ + B_Q_RANK
OFF_KR = OFF_CKV + B_KV_RANK
IN_W = OFF_KR + B_ROPE
IN_W_PAD = OFF_KR + LANES
POOL_WINDOWS = (2, 4, 8, 16)
POOL_HALO = SUBLANES
NEG_BIG = -1e30
LOG2_E = 1.4426950408889634
MLA_GROUP = 4

MOD_ROWS = 8


def _params(semantics, vmem_mib):
    return pltpu.CompilerParams(dimension_semantics=semantics, vmem_limit_bytes=vmem_mib * MIB)


def _const_spec(shape):
    nd = len(shape)
    return pl.BlockSpec(shape, lambda *_: (0,) * nd, pipeline_mode=pl.Buffered(1))


def _rms_mod(x, g, shift, scale):
    y = x * lax.rsqrt(jnp.mean(x * x, axis=-1, keepdims=True) + EPS)
    return (y * g) * (1.0 + scale) + shift


def _dot(a, b):
    return jnp.dot(a, b, preferred_element_type=F32)


def _dot_nt(a, b):
    return lax.dot_general(a, b, (((1,), (1,)), ((), ())), preferred_element_type=F32)


def _mods_kernel(c_ref, w_ref, b_ref, o_ref):
    c = c_ref[...]
    a = (c * jax.nn.sigmoid(c)).astype(BF16)
    o_ref[0] = _dot(a, w_ref[0].astype(BF16)) + b_ref[0]


def _mods(c, c_ctx, w_ada, b_ada):
    depth, d, nd = w_ada.shape
    b = c.shape[0]
    assert b + 1 <= MOD_ROWS
    cp = jnp.concatenate([c, c_ctx[None, :], jnp.zeros((MOD_ROWS - b - 1, d), F32)], axis=0)
    tn = min(d, 1024)
    assert nd % tn == 0
    out = pl.pallas_call(
        _mods_kernel,
        grid=(depth, nd // tn),
        in_specs=[
            pl.BlockSpec((MOD_ROWS, d), lambda l, j: (0, 0)),
            pl.BlockSpec((1, d, tn), lambda l, j: (l, 0, j)),
            pl.BlockSpec((1, 1, tn), lambda l, j: (l, 0, j)),
        ],
        out_specs=pl.BlockSpec((1, MOD_ROWS, tn), lambda l, j: (l, 0, j)),
        out_shape=jax.ShapeDtypeStruct((depth, MOD_ROWS, nd), F32),
        compiler_params=_params(("arbitrary", "arbitrary"), 40),
        name="adaln_mods",
    )(cp, w_ada, b_ada.reshape(depth, 1, nd))
    return out.reshape(depth, MOD_ROWS * N_MOD, 1, d)


def _mod_spec(d, row_of_tile, k):
    return pl.BlockSpec((1, 1, d), lambda i, *_: (row_of_tile(i) * N_MOD + k, 0, 0))


def _ffn_kernel(x_ref, g_ref, sh_ref, sc_ref, gt_ref, wg_ref, wu_ref, wd_ref, fn_ref,
                o_ref, h_ref, *, final_norm):
    j = pl.program_id(1)

    @pl.when(j == 0)
    def _():
        x = x_ref[...]
        h_ref[...] = _rms_mod(x, g_ref[...], sh_ref[0], sc_ref[0]).astype(BF16)
        o_ref[...] = x

    h = h_ref[...]
    gate = _dot(h, wg_ref[...])
    up = _dot(h, wu_ref[...])
    a = (gate * jax.nn.sigmoid(gate)) * up
    o_ref[...] += (0.5 * gt_ref[0]) * _dot(a.astype(BF16), wd_ref[...])

    if final_norm:
        @pl.when(j == pl.num_programs(1) - 1)
        def _():
            out = o_ref[...]
            o_ref[...] = out * lax.rsqrt(jnp.mean(out * out, axis=-1, keepdims=True) + EPS) * fn_ref[...]


def _ffn(x, mods, mod_base, row_of_tile, g, wg, wu, wd, fn, *, tm, tf, final_norm):
    t, d = x.shape
    f = wg.shape[1]
    return pl.pallas_call(
        functools.partial(_ffn_kernel, final_norm=final_norm),
        grid=(t // tm, f // tf),
        in_specs=[
            pl.BlockSpec((tm, d), lambda i, j: (i, 0)),
            pl.BlockSpec((1, d), lambda i, j: (0, 0)),
            _mod_spec(d, row_of_tile, mod_base),
            _mod_spec(d, row_of_tile, mod_base + 1),
            _mod_spec(d, row_of_tile, mod_base + 2),
            pl.BlockSpec((d, tf), lambda i, j: (0, j)),
            pl.BlockSpec((d, tf), lambda i, j: (0, j)),
            pl.BlockSpec((tf, d), lambda i, j: (j, 0)),
            pl.BlockSpec((1, d), lambda i, j: (0, 0)),
        ],
        out_specs=pl.BlockSpec((tm, d), lambda i, j: (i, 0)),
        out_shape=jax.ShapeDtypeStruct((t, d), F32),
        scratch_shapes=[pltpu.VMEM((tm, d), BF16)],
        compiler_params=_params(("parallel", "arbitrary"), 63),
        name="swiglu_half_step",
    )(x, g, mods, mods, mods, wg, wu, wd, fn)


def _rope_a(x, cos, sin):
    return x * cos + pltpu.roll(x, A_HEAD_DIM // 2, 1) * sin


def _rope_b(x, cos, sin_lo, sin_hi):
    half = B_ROPE // 2
    return x * cos + pltpu.roll(x, LANES - half, 1) * sin_lo + pltpu.roll(x, half, 1) * sin_hi


def _inproj_kernel(x_ref, g_ref, sh_ref, sc_ref, win_ref, qn_ref, wuq_ref, kvn_ref, wuk_ref, wuvt_ref,
                   cosa_ref, sina_ref, cosb_ref, sinlo_ref, sinhi_ref,
                   qa_ref, ka_ref, va_ref, qb_ref, kb_ref, vbt_ref):
    h = _rms_mod(x_ref[...], g_ref[...], sh_ref[0], sc_ref[0]).astype(BF16)
    cosa, sina = cosa_ref[...], sina_ref[...]
    cosb, sinlo, sinhi = cosb_ref[...], sinlo_ref[...], sinhi_ref[...]

    aq = _dot(h, win_ref[:, 0:OFF_AK])
    for s in range(A_HEADS):
        sl = slice(s * LANES, (s + 1) * LANES)
        qa_ref[:, sl] = _rope_a(aq[:, sl], cosa, sina).astype(BF16)
    akv = _dot(h, win_ref[:, OFF_AK:OFF_CQ])
    for s in range(A_KV_HEADS):
        sl = slice(s * LANES, (s + 1) * LANES)
        ka_ref[:, sl] = _rope_a(akv[:, sl], cosa, sina).astype(BF16)
    va_ref[...] = akv[:, A_KV_W:].astype(BF16)

    cq = _dot(h, win_ref[:, OFF_CQ:OFF_CKV])
    cqn = (cq * lax.rsqrt(jnp.mean(cq * cq, axis=-1, keepdims=True) + EPS) * qn_ref[...]).astype(BF16)
    qb = _dot(cqn, wuq_ref[...]) * (B_QK ** -0.5 * LOG2_E)
    for hd in range(B_HEADS):
        lo = hd * B_PAD
        qb_ref[:, lo:lo + LANES] = qb[:, lo:lo + LANES].astype(BF16)
        qb_ref[:, lo + LANES:lo + B_PAD] = _rope_b(qb[:, lo + LANES:lo + B_PAD], cosb, sinlo, sinhi).astype(BF16)

    ckr = _dot(h, win_ref[:, OFF_CKV:IN_W_PAD])
    ckv = ckr[:, 0:B_KV_RANK]
    ckvn = (ckv * lax.rsqrt(jnp.mean(ckv * ckv, axis=-1, keepdims=True) + EPS) * kvn_ref[...]).astype(BF16)
    kr = _rope_b(ckr[:, B_KV_RANK:], cosb, sinlo, sinhi).astype(BF16)
    kn = _dot(ckvn, wuk_ref[...])
    for hd in range(B_HEADS):
        lo = hd * B_PAD
        kb_ref[:, lo:lo + LANES] = kn[:, hd * B_NOPE:(hd + 1) * B_NOPE].astype(BF16)
        kb_ref[:, lo + LANES:lo + B_PAD] = kr
    vbt_ref[0] = _dot_nt(wuvt_ref[...], ckvn).astype(BF16)


def _inproj(x, mods, row_of_tile, g, win, qn, wuq, kvn, wuk, wuvt, tables, table_tile, *, tm):
    t, d = x.shape
    tok = lambda w: pl.BlockSpec((tm, w), lambda i: (i, 0))
    tab = pl.BlockSpec((tm, LANES), lambda i: (table_tile(i), 0))
    widths = (A_Q_W, A_KV_W, A_KV_W, B_HEADS * B_PAD, B_HEADS * B_PAD)
    vt_rows = B_HEADS * B_V
    return pl.pallas_call(
        _inproj_kernel,
        grid=(t // tm,),
        in_specs=[
            tok(d),
            _const_spec((1, d)),
            _mod_spec(d, row_of_tile, 3),
            _mod_spec(d, row_of_tile, 4),
            _const_spec(win.shape),
            _const_spec(qn.shape),
            _const_spec(wuq.shape),
            _const_spec(kvn.shape),
            _const_spec(wuk.shape),
            _const_spec(wuvt.shape),
            tab, tab, tab, tab, tab,
        ],
        out_specs=[tok(w) for w in widths] + [pl.BlockSpec((1, vt_rows, tm), lambda i: (i, 0, 0))],
        out_shape=[jax.ShapeDtypeStruct((t, w), BF16) for w in widths]
        + [jax.ShapeDtypeStruct((t // tm, vt_rows, tm), BF16)],
        compiler_params=_params(("parallel",), 56),
        name="attn_in_proj",
    )(x, g, mods, mods, win, qn, wuq, kvn, wuk, wuvt, *tables)


def _window_kernel(sink_ref, q_ref, k_ref, v_ref, kc_ref, vc_ref, o_ref, *, tq, n):
    qi = pl.program_id(1)
    span = tq + 2 * A_WINDOW
    start = pl.multiple_of(jnp.clip(qi * tq - A_WINDOW, 0, n - span), A_WINDOW)
    qpos = qi * tq + lax.broadcasted_iota(jnp.int32, (tq, span), 0)
    kpos = start + lax.broadcasted_iota(jnp.int32, (tq, span), 1)
    valid = jnp.abs(kpos - qpos) <= A_WINDOW
    scale = A_HEAD_DIM ** -0.5
    for g in range(A_KV_HEADS):
        gl = slice(g * A_HEAD_DIM, (g + 1) * A_HEAD_DIM)
        k = k_ref[0, pl.ds(start, span), gl]
        v = v_ref[0, pl.ds(start, span), gl]
        kc = kc_ref[0, :, gl]
        vc = vc_ref[0, :, gl]
        for r in range(A_REP):
            hd = g * A_REP + r
            hl = slice(hd * A_HEAD_DIM, (hd + 1) * A_HEAD_DIM)
            q = q_ref[0, :, hl]
            s_loc = jnp.where(valid, _dot_nt(q, k) * scale, NEG_BIG)
            s_ctx = _dot_nt(q, kc) * scale
            sink = sink_ref[hd]
            m = jnp.maximum(jnp.maximum(jnp.max(s_loc, axis=-1, keepdims=True),
                                        jnp.max(s_ctx, axis=-1, keepdims=True)), sink)
            p_loc = jnp.exp(s_loc - m)
            p_ctx = jnp.exp(s_ctx - m)
            denom = (jnp.sum(p_loc, axis=-1, keepdims=True) + jnp.sum(p_ctx, axis=-1, keepdims=True)
                     + jnp.exp(sink - m))
            o = _dot(p_loc.astype(BF16), v) + _dot(p_ctx.astype(BF16), vc)
            o_ref[0, :, hl] = (o / denom).astype(BF16)


def _window_gqa(qa, ka, va, kca, vca, sink, *, tq):
    b, n, _ = qa.shape
    nc = kca.shape[1]
    assert n >= tq + 2 * A_WINDOW
    return pl.pallas_call(
        functools.partial(_window_kernel, tq=tq, n=n),
        grid=(b, n // tq),
        in_specs=[
            pl.BlockSpec(memory_space=pltpu.SMEM),
            pl.BlockSpec((1, tq, A_Q_W), lambda bi, qi: (bi, qi, 0)),
            pl.BlockSpec((1, n, A_KV_W), lambda bi, qi: (bi, 0, 0)),
            pl.BlockSpec((1, n, A_KV_W), lambda bi, qi: (bi, 0, 0)),
            pl.BlockSpec((1, nc, A_KV_W), lambda bi, qi: (bi, 0, 0)),
            pl.BlockSpec((1, nc, A_KV_W), lambda bi, qi: (bi, 0, 0)),
        ],
        out_specs=pl.BlockSpec((1, tq, A_Q_W), lambda bi, qi: (bi, qi, 0)),
        out_shape=jax.ShapeDtypeStruct((b, n, A_Q_W), BF16),
        compiler_params=_params(("parallel", "arbitrary"), 48),
        name="window_gqa",
    )(sink, qa, ka, va, kca, vca)


def _mla_kernel(q_ref, k_ref, vt_ref, kc_ref, vct_ref, o_ref, sa_ref, sb_ref, *, tk):
    q = q_ref[0]
    nchunks = vt_ref.shape[1]

    def scores(c):
        return _dot_nt(k_ref[0, pl.ds(pl.multiple_of(c * tk, tk), tk), :], q)

    def update(s, vt, carry):
        m_old, l_old, acc_old = carry
        m = jnp.maximum(m_old, jnp.max(s, axis=0, keepdims=True))
        p = jnp.exp2(s - m)
        alpha = jnp.exp2(m_old - m)
        l = alpha * l_old + jnp.sum(p, axis=0, keepdims=True)
        acc = alpha * acc_old + _dot(vt, p.astype(BF16))
        return m, l, acc

    sa_ref[...] = scores(0)
    s = _dot_nt(kc_ref[0], q)
    m = jnp.max(s, axis=0, keepdims=True)
    p = jnp.exp2(s - m)
    carry = (m, jnp.sum(p, axis=0, keepdims=True), _dot(vct_ref[0, 0], p.astype(BF16)))

    bufs = (sa_ref, sb_ref)

    def group(c0, carry, last):
        for j in range(MLA_GROUP):
            if not (last and j == MLA_GROUP - 1):
                bufs[(j + 1) % 2][...] = scores(c0 + j + 1)
            carry = update(bufs[j % 2][...], vt_ref[0, c0 + j], carry)
        return carry

    carry = lax.fori_loop(0, nchunks // MLA_GROUP - 1,
                          lambda i, cr: group(i * MLA_GROUP, cr, False), carry)
    m, l, acc = group(nchunks - MLA_GROUP, carry, True)
    o_ref[0] = jnp.transpose(acc / l).astype(BF16)


def _mla_attention(qb, kb, vbt, kcb, vcbt, *, tq):
    b, n, _ = qb.shape
    nc = kcb.shape[1]
    nchunks, _, tk = vbt.shape[1:]
    assert nchunks * tk == n and nchunks % MLA_GROUP == 0 and vcbt.shape[1] == 1 and vcbt.shape[3] == nc
    return pl.pallas_call(
        functools.partial(_mla_kernel, tk=tk),
        grid=(b, B_HEADS, n // tq),
        in_specs=[
            pl.BlockSpec((1, tq, B_PAD), lambda bi, hi, qi: (bi, qi, hi)),
            pl.BlockSpec((1, n, B_PAD), lambda bi, hi, qi: (bi, 0, hi)),
            pl.BlockSpec((1, nchunks, B_V, tk), lambda bi, hi, qi: (bi, 0, hi, 0)),
            pl.BlockSpec((1, nc, B_PAD), lambda bi, hi, qi: (bi, 0, hi)),
            pl.BlockSpec((1, 1, B_V, nc), lambda bi, hi, qi: (bi, 0, hi, 0)),
        ],
        out_specs=pl.BlockSpec((1, tq, B_V), lambda bi, hi, qi: (bi, qi, hi)),
        out_shape=jax.ShapeDtypeStruct((b, n, B_HEADS * B_V), BF16),
        scratch_shapes=[pltpu.VMEM((tk, tq), F32), pltpu.VMEM((tk, tq), F32)],
        compiler_params=_params(("parallel", "parallel", "arbitrary"), 48),
        name="mla_attention",
    )(qb, kb, vbt, kcb, vcbt)


def _outproj_kernel(x_ref, oa_ref, ob_ref, w_ref, gt_ref, o_ref):
    ka = oa_ref.shape[1]
    y = _dot(oa_ref[...], w_ref[0:ka, :]) + _dot(ob_ref[...], w_ref[ka:, :])
    o_ref[...] = x_ref[...] + gt_ref[0] * y


def _outproj(x, oa, ob, w, mods, row_of_tile, *, tm):
    t, d = x.shape
    return pl.pallas_call(
        _outproj_kernel,
        grid=(t // tm,),
        in_specs=[
            pl.BlockSpec((tm, d), lambda i: (i, 0)),
            pl.BlockSpec((tm, oa.shape[1]), lambda i: (i, 0)),
            pl.BlockSpec((tm, ob.shape[1]), lambda i: (i, 0)),
            _const_spec(w.shape),
            _mod_spec(d, row_of_tile, 5),
        ],
        out_specs=pl.BlockSpec((tm, d), lambda i: (i, 0)),
        out_shape=jax.ShapeDtypeStruct((t, d), F32),
        compiler_params=_params(("parallel",), 48),
        name="attn_out_proj",
    )(x, oa, ob, w, mods)


def _pool_kernel(x_ref, xp_ref, xn_ref, g_ref, sh_ref, sc_ref, gt_ref, w_ref, ps_ref, o_ref, hp_ref,
                 *, tm, n):
    i = pl.program_id(1)
    last = pl.num_programs(1) - 1
    g, sh, sc = g_ref[...], sh_ref[0], sc_ref[0]
    x = x_ref[0]
    hp_ref[0:POOL_HALO, :] = jnp.where(i > 0, _rms_mod(xp_ref[0], g, sh, sc), 0.0)
    hp_ref[POOL_HALO:POOL_HALO + tm, :] = _rms_mod(x, g, sh, sc)
    hp_ref[POOL_HALO + tm:, :] = jnp.where(i < last, _rms_mod(xn_ref[0], g, sh, sc), 0.0)

    t = i * tm + lax.broadcasted_iota(jnp.int32, (tm, 1), 0)
    gw = x.shape[1] // len(POOL_WINDOWS)
    for gi, w in enumerate(POOL_WINDOWS):
        cols = slice(gi * gw, (gi + 1) * gw)
        tot = hp_ref[pl.ds(POOL_HALO - w // 2, tm), cols]
        for k in range(1, w):
            tot = tot + hp_ref[pl.ds(POOL_HALO - w // 2 + k, tm), cols]
        cnt = (jnp.minimum(t - w // 2 + w, n) - jnp.maximum(t - w // 2, 0)).astype(F32)
        diff = tot / cnt - hp_ref[pl.ds(POOL_HALO, tm), cols]
        y = _dot(diff.astype(BF16), w_ref[gi]) * ps_ref[:, cols]
        o_ref[0, :, cols] = x[:, cols] + gt_ref[0][:, cols] * y


def _pool_mixer(x3, mods, g, w, ps, *, tm):
    b, n, d = x3.shape
    hb = tm // POOL_HALO
    mod = lambda k: pl.BlockSpec((1, 1, d), lambda bi, i: (bi * N_MOD + k, 0, 0))
    return pl.pallas_call(
        functools.partial(_pool_kernel, tm=tm, n=n),
        grid=(b, n // tm),
        in_specs=[
            pl.BlockSpec((1, tm, d), lambda bi, i: (bi, i, 0)),
            pl.BlockSpec((1, POOL_HALO, d), lambda bi, i: (bi, jnp.maximum(i * hb - 1, 0), 0)),
            pl.BlockSpec((1, POOL_HALO, d), lambda bi, i: (bi, jnp.minimum((i + 1) * hb, n // POOL_HALO - 1), 0)),
            pl.BlockSpec((1, d), lambda bi, i: (0, 0)),
            mod(3), mod(4), mod(5),
            pl.BlockSpec(w.shape, lambda bi, i: (0, 0, 0)),
            pl.BlockSpec((1, d), lambda bi, i: (0, 0)),
        ],
        out_specs=pl.BlockSpec((1, tm, d), lambda bi, i: (bi, i, 0)),
        out_shape=jax.ShapeDtypeStruct((b, n, d), F32),
        scratch_shapes=[pltpu.VMEM((tm + 2 * POOL_HALO, d), F32)],
        compiler_params=_params(("parallel", "arbitrary"), 48),
        name="pool_mixer",
    )(x3, x3, x3, g, mods, mods, mods, w, ps)


def _rope_tables(n):
    rows = n // GRID_W
    row = jnp.repeat(jnp.arange(rows), GRID_W).astype(F32)
    col = jnp.tile(jnp.arange(GRID_W), rows).astype(F32)

    def angles(rot_dim):
        nf = rot_dim // 4
        inv = ROPE_BASE ** (-jnp.arange(nf, dtype=F32) / nf)
        return jnp.concatenate([row[:, None] * inv, col[:, None] * inv], axis=-1)

    ang_a, ang_b = angles(A_HEAD_DIM), angles(B_ROPE)
    ca, sa = jnp.cos(ang_a), jnp.sin(ang_a)
    cb, sb = jnp.cos(ang_b), jnp.sin(ang_b)
    zb = jnp.zeros_like(cb)
    zpad = jnp.zeros((n, LANES - B_ROPE), F32)
    return (jnp.concatenate([ca, ca], axis=-1), jnp.concatenate([-sa, sa], axis=-1),
            jnp.concatenate([cb, cb, zpad], axis=-1), jnp.concatenate([-sb, zb, zpad], axis=-1),
            jnp.concatenate([zb, sb, zpad], axis=-1))


def _identity_tables(n):
    one = jnp.ones((n, LANES), F32)
    zero = jnp.zeros((n, LANES), F32)
    return (one, zero, one, zero, zero)


def _attn_weights(w_in, w_uq, w_ukv):
    d = w_in.shape[0]
    win = jnp.concatenate([w_in, jnp.zeros((d, IN_W_PAD - IN_W), w_in.dtype)], axis=1).astype(BF16)
    uq = w_uq.reshape(B_Q_RANK, B_HEADS, B_QK)
    uq = jnp.concatenate([uq, jnp.zeros((B_Q_RANK, B_HEADS, B_PAD - B_QK), uq.dtype)], axis=-1)
    ukv = w_ukv.reshape(B_KV_RANK, B_HEADS, B_NOPE + B_V)
    uk = ukv[:, :, :B_NOPE].reshape(B_KV_RANK, B_HEADS * B_NOPE)
    uvt = ukv[:, :, B_NOPE:].reshape(B_KV_RANK, B_HEADS * B_V).T
    return win, uq.reshape(B_Q_RANK, B_HEADS * B_PAD).astype(BF16), uk.astype(BF16), uvt.astype(BF16)


def kernel(x, c, ctx, c_ctx, w_ada, b_ada, norm_ffn1, norm_mix, norm_ffn2, ffn1_w_gate, ffn1_w_up, ffn1_w_down,
           ffn2_w_gate, ffn2_w_up, ffn2_w_down, attn_w_in, attn_sink, mla_q_norm, mla_w_uq, mla_kv_norm, mla_w_ukv,
           attn_w_out, pool_w, pool_scale, final_norm):
    b, n, d = x.shape
    nc = ctx.shape[1]
    depth = w_ada.shape[0]
    f = ffn1_w_gate.shape[2]
    t = b * n

    tm = min(512, n)
    tm_ffn = min(1024, n)
    tf = min(512, f)
    tmc = min(256, nc)
    tq_a = min(256, n)
    tq_b = min(1024, n)
    assert n % tm == 0 and f % tf == 0 and nc % tmc == 0 and n % GRID_W == 0

    tiles_per_sample = n // tm
    x_row = lambda i: i // tiles_per_sample
    ffn_row = lambda i: i // (n // tm_ffn)
    ctx_row = lambda i: b

    mods = _mods(c, c_ctx, w_ada, b_ada)
    rope = _rope_tables(n)
    fn = final_norm.reshape(1, d)

    xs = x.reshape(t, d)
    cx = ctx.reshape(b * nc, d)
    for l in range(depth):
        is_attn = l % 2 == 0
        ctx_out = any(j % 2 == 0 for j in range(l + 1, depth))
        ctx_in = is_attn or ctx_out
        i = l // 2
        m = mods[l]
        g1, gm, g2 = (v[l].reshape(1, d) for v in (norm_ffn1, norm_mix, norm_ffn2))
        w1 = tuple(w[l].astype(BF16) for w in (ffn1_w_gate, ffn1_w_up, ffn1_w_down))
        w2 = tuple(w[l].astype(BF16) for w in (ffn2_w_gate, ffn2_w_up, ffn2_w_down))

        xs = _ffn(xs, m, 0, ffn_row, g1, *w1, fn, tm=tm_ffn, tf=tf, final_norm=False)
        if ctx_in:
            cx = _ffn(cx, m, 0, ctx_row, g1, *w1, fn, tm=tmc, tf=tf, final_norm=False)

        if is_attn:
            win, wuq, wuk, wuvt = _attn_weights(attn_w_in[i], mla_w_uq[i], mla_w_ukv[i])
            qn = mla_q_norm[i].reshape(1, -1)
            kvn = mla_kv_norm[i].reshape(1, -1)
            qa, ka, va, qb, kb, vbt = _inproj(xs, m, x_row, gm, win, qn, wuq, kvn, wuk, wuvt, rope,
                                              lambda ti: ti % tiles_per_sample, tm=tm)
            _, kca, vca, _, kcb, vcbt = _inproj(cx, m, ctx_row, gm, win, qn, wuq, kvn, wuk, wuvt,
                                                _identity_tables(nc), lambda ti: 0, tm=nc)
            r3 = lambda a, rows: a.reshape(b, rows, a.shape[-1])
            r4 = lambda a: a.reshape(b, a.shape[0] // b, a.shape[1], a.shape[2])
            oa = _window_gqa(r3(qa, n), r3(ka, n), r3(va, n), r3(kca, nc), r3(vca, nc), attn_sink[i], tq=tq_a)
            ob = _mla_attention(r3(qb, n), r3(kb, n), r4(vbt), r3(kcb, nc), r4(vcbt), tq=tq_b)
            xs = _outproj(xs, oa.reshape(t, -1), ob.reshape(t, -1), attn_w_out[i].astype(BF16), m, x_row, tm=tm)
            if ctx_out:
                raise NotImplementedError("context-stream attention output is not needed at this depth")
        else:
            xs = _pool_mixer(xs.reshape(b, n, d), m, gm, pool_w[i].astype(BF16), pool_scale[i].reshape(1, d),
                             tm=tm).reshape(t, d)
            if ctx_out:
                raise NotImplementedError("context-stream pooling output is not needed at this depth")

        xs = _ffn(xs, m, 6, ffn_row, g2, *w2, fn, tm=tm_ffn, tf=tf, final_norm=(l == depth - 1))
        if ctx_out:
            cx = _ffn(cx, m, 6, ctx_row, g2, *w2, fn, tm=tmc, tf=tf, final_norm=False)
    return xs.reshape(b, n, d)
```

```python
import functools

import jax
import jax.numpy as jnp
from jax import lax
from jax.experimental import pallas as pl
from jax.experimental.pallas import tpu as pltpu

F32 = jnp.float32
BF16 = jnp.bfloat16

LANES = 128
SUBLANES = 8
BF16_SUBLANES = 16
MIB = 1 << 20

EPS = 1e-6
ROPE_BASE = 10000.0
GRID_W = 64
N_MOD = 9
A_HEADS = 8
A_KV_HEADS = 2
A_REP = A_HEADS // A_KV_HEADS
A_HEAD_DIM = 128
A_WINDOW = 128
B_HEADS = 8
B_NOPE = 128
B_ROPE = 64
B_QK = B_NOPE + B_ROPE
B_V = 128
B_Q_RANK = 768
B_KV_RANK = 256
B_PAD = 2 * LANES
A_Q_W = A_HEADS * A_HEAD_DIM
A_KV_W = A_KV_HEADS * A_HEAD_DIM
OFF_AK = A_Q_W
OFF_AV = OFF_AK + A_KV_W
OFF_CQ = OFF_AV + A_KV_W
OFF_CKV = OFF_CQ + B_Q_RANK
OFF_KR = OFF_CKV + B_KV_RANK
IN_W = OFF_KR + B_ROPE
IN_W_PAD = OFF_KR + LANES
POOL_WINDOWS = (2, 4, 8, 16)
POOL_HALO = SUBLANES
NEG_BIG = -1e30
LOG2_E = 1.4426950408889634
MLA_GROUP = 4

MOD_ROWS = 8


def _params(semantics, vmem_mib):
    return pltpu.CompilerParams(dimension_semantics=semantics, vmem_limit_bytes=vmem_mib * MIB)


def _const_spec(shape):
    nd = len(shape)
    return pl.BlockSpec(shape, lambda *_: (0,) * nd, pipeline_mode=pl.Buffered(1))


def _rms_mod(x, g, shift, scale):
    y = x * lax.rsqrt(jnp.mean(x * x, axis=-1, keepdims=True) + EPS)
    return (y * g) * (1.0 + scale) + shift


def _dot(a, b):
    return jnp.dot(a, b, preferred_element_type=F32)


def _dot_nt(a, b):
    return lax.dot_general(a, b, (((1,), (1,)), ((), ())), preferred_element_type=F32)


def _dot_tn(a, b):
    return lax.dot_general(a, b, (((0,), (0,)), ((), ())), preferred_element_type=F32)


def _mods_kernel(c_ref, w_ref, b_ref, o_ref):
    c = c_ref[...]
    a = (c * jax.nn.sigmoid(c)).astype(BF16)
    o_ref[0] = _dot(a, w_ref[0].astype(BF16)) + b_ref[0]


def _mods(c, c_ctx, w_ada, b_ada):
    depth, d, nd = w_ada.shape
    b = c.shape[0]
    assert b + 1 <= MOD_ROWS
    cp = jnp.concatenate([c, c_ctx[None, :], jnp.zeros((MOD_ROWS - b - 1, d), F32)], axis=0)
    tn = min(d, 1024)
    assert nd % tn == 0
    out = pl.pallas_call(
        _mods_kernel,
        grid=(depth, nd // tn),
        in_specs=[
            pl.BlockSpec((MOD_ROWS, d), lambda l, j: (0, 0)),
            pl.BlockSpec((1, d, tn), lambda l, j: (l, 0, j)),
            pl.BlockSpec((1, 1, tn), lambda l, j: (l, 0, j)),
        ],
        out_specs=pl.BlockSpec((1, MOD_ROWS, tn), lambda l, j: (l, 0, j)),
        out_shape=jax.ShapeDtypeStruct((depth, MOD_ROWS, nd), F32),
        compiler_params=_params(("arbitrary", "arbitrary"), 40),
        name="adaln_mods",
    )(cp, w_ada, b_ada.reshape(depth, 1, nd))
    return out.reshape(depth, MOD_ROWS * N_MOD, 1, d)


def _mod_spec(d, row_of_tile, k):
    return pl.BlockSpec((1, 1, d), lambda i, *_: (row_of_tile(i) * N_MOD + k, 0, 0))


def _ffn_kernel(x_ref, g_ref, sh_ref, sc_ref, gt_ref, wg_ref, wu_ref, wd_ref, fn_ref,
                o_ref, h_ref, *, final_norm):
    j = pl.program_id(1)

    @pl.when(j == 0)
    def _():
        x = x_ref[...]
        h_ref[...] = _rms_mod(x, g_ref[...], sh_ref[0], sc_ref[0]).astype(BF16)
        o_ref[...] = x

    h = h_ref[...]
    gate = _dot(h, wg_ref[...])
    up = _dot(h, wu_ref[...])
    a = (gate * jax.nn.sigmoid(gate)) * up
    o_ref[...] += (0.5 * gt_ref[0]) * _dot(a.astype(BF16), wd_ref[...])

    if final_norm:
        @pl.when(j == pl.num_programs(1) - 1)
        def _():
            out = o_ref[...]
            o_ref[...] = out * lax.rsqrt(jnp.mean(out * out, axis=-1, keepdims=True) + EPS) * fn_ref[...]


def _ffn(x, mods, mod_base, row_of_tile, g, wg, wu, wd, fn, *, tm, tf, final_norm):
    t, d = x.shape
    f = wg.shape[1]
    return pl.pallas_call(
        functools.partial(_ffn_kernel, final_norm=final_norm),
        grid=(t // tm, f // tf),
        in_specs=[
            pl.BlockSpec((tm, d), lambda i, j: (i, 0)),
            pl.BlockSpec((1, d), lambda i, j: (0, 0)),
            _mod_spec(d, row_of_tile, mod_base),
            _mod_spec(d, row_of_tile, mod_base + 1),
            _mod_spec(d, row_of_tile, mod_base + 2),
            pl.BlockSpec((d, tf), lambda i, j: (0, j)),
            pl.BlockSpec((d, tf), lambda i, j: (0, j)),
            pl.BlockSpec((tf, d), lambda i, j: (j, 0)),
            pl.BlockSpec((1, d), lambda i, j: (0, 0)),
        ],
        out_specs=pl.BlockSpec((tm, d), lambda i, j: (i, 0)),
        out_shape=jax.ShapeDtypeStruct((t, d), F32),
        scratch_shapes=[pltpu.VMEM((tm, d), BF16)],
        compiler_params=_params(("parallel", "arbitrary"), 63),
        name="swiglu_half_step",
    )(x, g, mods, mods, mods, wg, wu, wd, fn)


def _rope_a(x, cos, sin):
    return x * cos + pltpu.roll(x, A_HEAD_DIM // 2, 1) * sin


def _rope_b(x, cos, sin_lo, sin_hi):
    half = B_ROPE // 2
    return x * cos + pltpu.roll(x, LANES - half, 1) * sin_lo + pltpu.roll(x, half, 1) * sin_hi


def _inproj_kernel(x_ref, g_ref, sh_ref, sc_ref, win_ref, qn_ref, wuq_ref, kvn_ref, wuk_ref, wuvt_ref,
                   cosa_ref, sina_ref, cosb_ref, sinlo_ref, sinhi_ref,
                   qa_ref, ka_ref, va_ref, qb_ref, kb_ref, vbt_ref):
    h = _rms_mod(x_ref[...], g_ref[...], sh_ref[0], sc_ref[0]).astype(BF16)
    cosa, sina = cosa_ref[...], sina_ref[...]
    cosb, sinlo, sinhi = cosb_ref[...], sinlo_ref[...], sinhi_ref[...]

    aq = _dot(h, win_ref[:, 0:OFF_AK]) * (A_HEAD_DIM ** -0.5 * LOG2_E)
    for s in range(A_HEADS):
        sl = slice(s * LANES, (s + 1) * LANES)
        qa_ref[:, sl] = _rope_a(aq[:, sl], cosa, sina).astype(BF16)
    akv = _dot(h, win_ref[:, OFF_AK:OFF_CQ])
    for s in range(A_KV_HEADS):
        sl = slice(s * LANES, (s + 1) * LANES)
        ka_ref[:, sl] = _rope_a(akv[:, sl], cosa, sina).astype(BF16)
    va_ref[...] = akv[:, A_KV_W:].astype(BF16)

    cq = _dot(h, win_ref[:, OFF_CQ:OFF_CKV])
    cqn = (cq * lax.rsqrt(jnp.mean(cq * cq, axis=-1, keepdims=True) + EPS) * qn_ref[...]).astype(BF16)
    qb =_dot(cqn, wuq_ref[...]) * (B_QK ** -0.5 * LOG2_E)
    for hd in range(B_HEADS):
        lo = hd * B_PAD
        qb_ref[:, lo:lo + LANES] = qb[:, lo:lo + LANES].astype(BF16)
        qb_ref[:, lo + LANES:lo + B_PAD] = _rope_b(qb[:, lo + LANES:lo + B_PAD], cosb, sinlo, sinhi).astype(BF16)

    ckr = _dot(h, win_ref[:, OFF_CKV:IN_W_PAD])
    ckv = ckr[:, 0:B_KV_RANK]
    ckvn = (ckv * lax.rsqrt(jnp.mean(ckv * ckv, axis=-1, keepdims=True) + EPS) * kvn_ref[...]).astype(BF16)
    kr = _rope_b(ckr[:, B_KV_RANK:], cosb, sinlo, sinhi).astype(BF16)
    kn = _dot(ckvn, wuk_ref[...])
    for hd in range(B_HEADS):
        lo = hd * B_PAD
        kb_ref[:, lo:lo + LANES] = kn[:, hd * B_NOPE:(hd + 1) * B_NOPE].astype(BF16)
        kb_ref[:, lo + LANES:lo + B_PAD] = kr
    vbt_ref[0] = _dot_nt(wuvt_ref[...], ckvn).astype(BF16)


def _inproj(x, mods, row_of_tile, g, win, qn, wuq, kvn, wuk, wuvt, tables, table_tile, *, tm):
    t, d = x.shape
    tok = lambda w: pl.BlockSpec((tm, w), lambda i: (i, 0))
    tab = pl.BlockSpec((tm, LANES), lambda i: (table_tile(i), 0))
    widths = (A_Q_W, A_KV_W, A_KV_W, B_HEADS * B_PAD, B_HEADS * B_PAD)
    vt_rows = B_HEADS * B_V
    return pl.pallas_call(
        _inproj_kernel,
        grid=(t // tm,),
        in_specs=[
            tok(d),
            _const_spec((1, d)),
            _mod_spec(d, row_of_tile, 3),
            _mod_spec(d, row_of_tile, 4),
            _const_spec(win.shape),
            _const_spec(qn.shape),
            _const_spec(wuq.shape),
            _const_spec(kvn.shape),
            _const_spec(wuk.shape),
            _const_spec(wuvt.shape),
            tab, tab, tab, tab, tab,
        ],
        out_specs=[tok(w) for w in widths] + [pl.BlockSpec((1, vt_rows, tm), lambda i: (i, 0, 0))],
        out_shape=[jax.ShapeDtypeStruct((t, w), BF16) for w in widths]
        + [jax.ShapeDtypeStruct((t // tm, vt_rows, tm), BF16)],
        compiler_params=_params(("parallel",), 56),
        name="attn_in_proj",
    )(x, g, mods, mods, win, qn, wuq, kvn, wuk, wuvt, *tables)


def _window_kernel(sink_ref, q_ref, k_ref, v_ref, kc_ref, vc_ref, o_ref, *, tq, n):
    qi = pl.program_id(1)
    span = tq + 2 * A_WINDOW
    start = pl.multiple_of(jnp.clip(qi * tq - A_WINDOW, 0, n - span), A_WINDOW)
    kpos = start + lax.broadcasted_iota(jnp.int32, (span, tq), 0)
    qpos = qi * tq + lax.broadcasted_iota(jnp.int32, (span, tq), 1)
    bias = jnp.where(jnp.abs(kpos - qpos) <= A_WINDOW, 0.0, NEG_BIG)
    for g in range(A_KV_HEADS):
        gl = slice(g * A_HEAD_DIM, (g + 1) * A_HEAD_DIM)
        heads = [slice((g * A_REP + r) * A_HEAD_DIM, (g * A_REP + r + 1) * A_HEAD_DIM) for r in range(A_REP)]
        k = k_ref[0, pl.ds(start, span), gl]
        v = v_ref[0, pl.ds(start, span), gl]
        q = jnp.concatenate([q_ref[0, :, hl] for hl in heads], axis=0)
        s_loc = _dot_nt(k, q)
        s_loc = jnp.concatenate([s_loc[:, r * tq:(r + 1) * tq] + bias for r in range(A_REP)], axis=1)
        s_ctx = _dot_nt(kc_ref[0, :, gl], q)
        sink = jnp.concatenate([jnp.full((1, tq), sink_ref[g * A_REP + r] * LOG2_E, F32) for r in range(A_REP)],
                               axis=1)
        m = jnp.maximum(jnp.maximum(jnp.max(s_loc, axis=0, keepdims=True),
                                    jnp.max(s_ctx, axis=0, keepdims=True)), sink)
        p_loc = jnp.exp2(s_loc - m)
        p_ctx = jnp.exp2(s_ctx - m)
        denom = (jnp.sum(p_loc, axis=0, keepdims=True) + jnp.sum(p_ctx, axis=0, keepdims=True)
                 + jnp.exp2(sink - m))
        o_t = _dot_tn(v, p_loc.astype(BF16)) + _dot_tn(vc_ref[0, :, gl], p_ctx.astype(BF16))
        o = jnp.transpose(o_t / denom)
        for r, hl in enumerate(heads):
            o_ref[0, :, hl] = o[r * tq:(r + 1) * tq].astype(BF16)


def _window_gqa(qa, ka, va, kca, vca, sink, *, tq):
    b, n, _ = qa.shape
    nc = kca.shape[1]
    assert n >= tq + 2 * A_WINDOW
    return pl.pallas_call(
        functools.partial(_window_kernel, tq=tq, n=n),
        grid=(b, n // tq),
        in_specs=[
            pl.BlockSpec(memory_space=pltpu.SMEM),
            pl.BlockSpec((1, tq, A_Q_W), lambda bi, qi: (bi, qi, 0)),
            pl.BlockSpec((1, n, A_KV_W), lambda bi, qi: (bi, 0, 0)),
            pl.BlockSpec((1, n, A_KV_W), lambda bi, qi: (bi, 0, 0)),
            pl.BlockSpec((1, nc, A_KV_W), lambda bi, qi: (bi, 0, 0)),
            pl.BlockSpec((1, nc, A_KV_W), lambda bi, qi: (bi, 0, 0)),
        ],
        out_specs=pl.BlockSpec((1, tq, A_Q_W), lambda bi, qi: (bi, qi, 0)),
        out_shape=jax.ShapeDtypeStruct((b, n, A_Q_W), BF16),
        compiler_params=_params(("parallel", "arbitrary"), 48),
        name="window_gqa",
    )(sink, qa, ka, va, kca, vca)


def _mla_kernel(q_ref, k_ref, vt_ref, kc_ref, vct_ref, *rest, tk, n_cast):
    cast_src, o_ref, cast_dst = rest[:n_cast], rest[n_cast], rest[n_cast + 1:2 * n_cast + 1]
    sa_ref, sb_ref, qt_ref = rest[2 * n_cast + 1:]
    for src, dst in zip(cast_src, cast_dst):
        dst[...] = src[...].astype(BF16)

    qt_ref[...] = q_ref[0].T
    qt = qt_ref[...]
    nchunks = vt_ref.shape[1]

    def scores(c):
        return _dot(k_ref[0, pl.ds(pl.multiple_of(c * tk, tk), tk), :], qt)

    def update(s, vt, carry):
        m_old, l_old, acc_old = carry
        m = jnp.maximum(m_old, jnp.max(s, axis=0, keepdims=True))
        p = jnp.exp2(s - m)
        alpha = jnp.exp2(m_old - m)
        l = alpha * l_old + jnp.sum(p, axis=0, keepdims=True)
        acc = alpha * acc_old + _dot(vt, p.astype(BF16))
        return m, l, acc

    sa_ref[...] = scores(0)
    s = _dot(kc_ref[0], qt)
    m = jnp.max(s, axis=0, keepdims=True)
    p = jnp.exp2(s - m)
    carry = (m, jnp.sum(p, axis=0, keepdims=True), _dot(vct_ref[0, 0], p.astype(BF16)))

    bufs = (sa_ref, sb_ref)

    def group(c0, carry, last):
        for j in range(MLA_GROUP):
            if not (last and j == MLA_GROUP - 1):
                bufs[(j + 1) % 2][...] = scores(c0 + j + 1)
            carry = update(bufs[j % 2][...], vt_ref[0, c0 + j], carry)
        return carry

    carry = lax.fori_loop(0, nchunks // MLA_GROUP - 1,
                          lambda i, cr: group(i * MLA_GROUP, cr, False), carry)
    m, l, acc = group(nchunks - MLA_GROUP, carry, True)
    o_ref[0] = jnp.transpose(acc / l).astype(BF16)


def _cast_split(rows, cols, steps):
    for cb in range(1, steps + 1):
        rb = steps // cb
        if (steps % cb == 0 and rows % rb == 0 and cols % cb == 0
                and (rows // rb) % BF16_SUBLANES == 0 and (cols // cb) % LANES == 0):
            return rb, cb
    raise ValueError(f"cannot split a ({rows}, {cols}) matrix into {steps} aligned blocks")


def _mla_attention(qb, kb, vbt, kcb, vcbt, casts, *, tq):
    b, n, _ = qb.shape
    nc = kcb.shape[1]
    nchunks, _, tk = vbt.shape[1:]
    nq = n // tq
    assert nchunks * tk == n and nchunks % MLA_GROUP == 0 and vcbt.shape[1] == 1 and vcbt.shape[3] == nc
    steps = b * B_HEADS * nq

    cast_in, cast_out, cast_shapes = [], [], []
    for w, layer in casts:
        _, rows, cols = w.shape
        rb, cb = _cast_split(rows, cols, steps)
        blk = (rows // rb, cols // cb)
        step = lambda bi, hi, qi: (bi * B_HEADS + hi) * nq + qi
        cast_in.append(pl.BlockSpec((None,) + blk,
                                    lambda bi, hi, qi, layer=layer, cb=cb: (layer, step(bi, hi, qi) // cb,
                                                                            step(bi, hi, qi) % cb)))
        cast_out.append(pl.BlockSpec(blk, lambda bi, hi, qi, cb=cb: (step(bi, hi, qi) // cb, step(bi, hi, qi) % cb)))
        cast_shapes.append(jax.ShapeDtypeStruct((rows, cols), BF16))

    out = pl.pallas_call(
        functools.partial(_mla_kernel, tk=tk, n_cast=len(casts)),
        grid=(b, B_HEADS, nq),
        in_specs=[
            pl.BlockSpec((1, tq, B_PAD), lambda bi, hi, qi: (bi, qi, hi)),
            pl.BlockSpec((1, n, B_PAD), lambda bi, hi, qi: (bi, 0, hi)),
            pl.BlockSpec((1, nchunks, B_V, tk), lambda bi, hi, qi: (bi, 0, hi, 0)),
            pl.BlockSpec((1, nc, B_PAD), lambda bi, hi, qi: (bi, 0, hi)),
            pl.BlockSpec((1, 1, B_V, nc), lambda bi, hi, qi: (bi, 0, hi, 0)),
        ] + cast_in,
        out_specs=[pl.BlockSpec((1, tq, B_V), lambda bi, hi, qi: (bi, qi, hi))] + cast_out,
        out_shape=[jax.ShapeDtypeStruct((b, n, B_HEADS * B_V), BF16)] + cast_shapes,
        scratch_shapes=[pltpu.VMEM((tk, tq), F32), pltpu.VMEM((tk, tq), F32), pltpu.VMEM((B_PAD, tq), BF16)],
        compiler_params=_params(("arbitrary", "arbitrary", "arbitrary"), 48),
        name="mla_attention",
    )(qb, kb, vbt, kcb, vcbt, *[w for w, _ in casts])
    return out[0], out[1:]


def _outproj_kernel(x_ref, oa_ref, ob_ref, w_ref, gt_ref, o_ref):
    ka = oa_ref.shape[1]
    y = _dot(oa_ref[...], w_ref[0:ka, :]) + _dot(ob_ref[...], w_ref[ka:, :])
    o_ref[...] = x_ref[...] + gt_ref[0] * y


def _outproj(x, oa, ob, w, mods, row_of_tile, *, tm):
    t, d = x.shape
    return pl.pallas_call(
        _outproj_kernel,
        grid=(t // tm,),
        in_specs=[
            pl.BlockSpec((tm, d), lambda i: (i, 0)),
            pl.BlockSpec((tm, oa.shape[1]), lambda i: (i, 0)),
            pl.BlockSpec((tm, ob.shape[1]), lambda i: (i, 0)),
            _const_spec(w.shape),
            _mod_spec(d, row_of_tile, 5),
        ],
        out_specs=pl.BlockSpec((tm, d), lambda i: (i, 0)),
        out_shape=jax.ShapeDtypeStruct((t, d), F32),
        compiler_params=_params(("parallel",), 48),
        name="attn_out_proj",
    )(x, oa, ob, w, mods)


def _pool_kernel(x_ref, xp_ref, xn_ref, g_ref, sh_ref, sc_ref, gt_ref, w_ref, ps_ref, o_ref, hp_ref,
                 *, tm, n):
    i = pl.program_id(1)
    last = pl.num_programs(1) - 1
    g, sh, sc = g_ref[...], sh_ref[0], sc_ref[0]
    x = x_ref[0]
    hp_ref[0:POOL_HALO, :] = jnp.where(i > 0, _rms_mod(xp_ref[0], g, sh, sc), 0.0)
    hp_ref[POOL_HALO:POOL_HALO + tm, :] = _rms_mod(x, g, sh, sc)
    hp_ref[POOL_HALO + tm:, :] = jnp.where(i < last, _rms_mod(xn_ref[0], g, sh, sc), 0.0)

    t = i * tm + lax.broadcasted_iota(jnp.int32, (tm, 1), 0)
    gw = x.shape[1] // len(POOL_WINDOWS)
    for gi, w in enumerate(POOL_WINDOWS):
        cols = slice(gi * gw, (gi + 1) * gw)
        rows = tm + 2 * POOL_HALO
        shift_up = lambda a, k: a if k == 0 else pltpu.roll(a, rows - k, 0)
        tot = shift_up(hp_ref[:, cols], POOL_HALO - w // 2)
        k = 1
        while k < w:
            tot = tot + shift_up(tot, k)
            k *= 2
        tot = tot[0:tm]
        cnt =(jnp.minimum(t - w // 2 + w, n) - jnp.maximum(t - w // 2, 0)).astype(F32)
        diff = tot / cnt - hp_ref[pl.ds(POOL_HALO, tm), cols]
        y = _dot(diff.astype(BF16), w_ref[gi]) * ps_ref[:, cols]
        o_ref[0, :, cols] = x[:, cols] + gt_ref[0][:, cols] * y


def _pool_mixer(x3, mods, g, w, ps, *, tm):
    b, n, d = x3.shape
    hb = tm // POOL_HALO
    mod = lambda k: pl.BlockSpec((1, 1, d), lambda bi, i: (bi * N_MOD + k, 0, 0))
    return pl.pallas_call(
        functools.partial(_pool_kernel, tm=tm, n=n),
        grid=(b, n // tm),
        in_specs=[
            pl.BlockSpec((1, tm, d), lambda bi, i: (bi, i, 0)),
            pl.BlockSpec((1, POOL_HALO, d), lambda bi, i: (bi, jnp.maximum(i * hb - 1, 0), 0)),
            pl.BlockSpec((1, POOL_HALO, d), lambda bi, i: (bi, jnp.minimum((i + 1) * hb, n // POOL_HALO - 1), 0)),
            pl.BlockSpec((1, d), lambda bi, i: (0, 0)),
            mod(3), mod(4), mod(5),
            pl.BlockSpec(w.shape, lambda bi, i: (0, 0, 0)),
            pl.BlockSpec((1, d), lambda bi, i: (0, 0)),
        ],
        out_specs=pl.BlockSpec((1, tm, d), lambda bi, i: (bi, i, 0)),
        out_shape=jax.ShapeDtypeStruct((b, n, d), F32),
        scratch_shapes=[pltpu.VMEM((tm + 2 * POOL_HALO, d), F32)],
        compiler_params=_params(("parallel", "arbitrary"), 48),
        name="pool_mixer",
    )(x3, x3, x3, g, mods, mods, mods, w, ps)


def _rope_tables(n):
    rows = n // GRID_W
    row = jnp.repeat(jnp.arange(rows), GRID_W).astype(F32)
    col = jnp.tile(jnp.arange(GRID_W), rows).astype(F32)

    def angles(rot_dim):
        nf = rot_dim // 4
        inv = ROPE_BASE ** (-jnp.arange(nf, dtype=F32) / nf)
        return jnp.concatenate([row[:, None] * inv, col[:, None] * inv], axis=-1)

    ang_a, ang_b = angles(A_HEAD_DIM), angles(B_ROPE)
    ca, sa = jnp.cos(ang_a), jnp.sin(ang_a)
    cb, sb = jnp.cos(ang_b), jnp.sin(ang_b)
    zb = jnp.zeros_like(cb)
    zpad = jnp.zeros((n, LANES - B_ROPE), F32)
    return (jnp.concatenate([ca, ca], axis=-1), jnp.concatenate([-sa, sa], axis=-1),
            jnp.concatenate([cb, cb, zpad], axis=-1), jnp.concatenate([-sb, zb, zpad], axis=-1),
            jnp.concatenate([zb, sb, zpad], axis=-1))


def _identity_tables(n):
    one = jnp.ones((n, LANES), F32)
    zero = jnp.zeros((n, LANES), F32)
    return (one, zero, one, zero, zero)


def _attn_weights(w_in, w_uq, w_ukv):
    d = w_in.shape[0]
    win = jnp.concatenate([w_in, jnp.zeros((d, IN_W_PAD - IN_W), w_in.dtype)], axis=1).astype(BF16)
    uq = w_uq.reshape(B_Q_RANK, B_HEADS, B_QK)
    uq = jnp.concatenate([uq, jnp.zeros((B_Q_RANK, B_HEADS, B_PAD - B_QK), uq.dtype)], axis=-1)
    ukv = w_ukv.reshape(B_KV_RANK, B_HEADS, B_NOPE + B_V)
    uk = ukv[:, :, :B_NOPE].reshape(B_KV_RANK, B_HEADS * B_NOPE)
    uvt = ukv[:, :, B_NOPE:].reshape(B_KV_RANK, B_HEADS * B_V).T
    return win, uq.reshape(B_Q_RANK, B_HEADS * B_PAD).astype(BF16), uk.astype(BF16), uvt.astype(BF16)


def kernel(x, c, ctx, c_ctx, w_ada, b_ada, norm_ffn1, norm_mix, norm_ffn2, ffn1_w_gate, ffn1_w_up, ffn1_w_down,
           ffn2_w_gate, ffn2_w_up, ffn2_w_down, attn_w_in, attn_sink, mla_q_norm, mla_w_uq, mla_kv_norm, mla_w_ukv,
           attn_w_out, pool_w, pool_scale, final_norm):
    b, n, d = x.shape
    nc = ctx.shape[1]
    depth = w_ada.shape[0]
    f = ffn1_w_gate.shape[2]
    t = b * n

    tm = min(512, n)
    tm_ffn = min(1024, n)
    tf = min(512, f)
    tmc = min(256, nc)
    tq_a = min(256, n)
    tq_b = min(1024, n)
    assert n % tm == 0 and f % tf == 0 and nc % tmc == 0 and n % GRID_W == 0

    tiles_per_sample = n // tm
    x_row = lambda i: i // tiles_per_sample
    ffn_row = lambda i: i // (n // tm_ffn)
    ctx_row = lambda i: b

    mods = _mods(c, c_ctx, w_ada, b_ada)
    rope = _rope_tables(n)
    fn = final_norm.reshape(1, d)

    ffn_f32 = ((ffn1_w_gate, ffn1_w_up, ffn1_w_down), (ffn2_w_gate, ffn2_w_up, ffn2_w_down))
    ffn_bf16 = {}

    def ffn_weights(family, layer):
        if (family, layer) not in ffn_bf16:
            ffn_bf16[(family, layer)] = tuple(w[layer].astype(BF16) for w in ffn_f32[family])
        return ffn_bf16[(family, layer)]

    xs = x.reshape(t, d)
    cx = ctx.reshape(b * nc, d)
    for l in range(depth):
        is_attn = l % 2 == 0
        ctx_out = any(j % 2 == 0 for j in range(l + 1, depth))
        ctx_in = is_attn or ctx_out
        i = l // 2
        m = mods[l]
        g1, gm, g2 = (v[l].reshape(1, d) for v in (norm_ffn1, norm_mix, norm_ffn2))
        w1 = ffn_weights(0, l)

        xs = _ffn(xs, m, 0, ffn_row, g1, *w1, fn, tm=tm_ffn, tf=tf, final_norm=False)
        if ctx_in:
            cx = _ffn(cx, m, 0, ctx_row, g1, *w1, fn, tm=tmc, tf=tf, final_norm=False)

        if is_attn:
            win, wuq, wuk, wuvt = _attn_weights(attn_w_in[i], mla_w_uq[i], mla_w_ukv[i])
            qn = mla_q_norm[i].reshape(1, -1)
            kvn = mla_kv_norm[i].reshape(1, -1)
            qa, ka, va, qb, kb, vbt = _inproj(xs, m, x_row, gm, win, qn, wuq, kvn, wuk, wuvt, rope,
                                              lambda ti: ti % tiles_per_sample, tm=tm)
            _, kca, vca, _, kcb, vcbt = _inproj(cx, m, ctx_row, gm, win, qn, wuq, kvn, wuk, wuvt,
                                                _identity_tables(nc), lambda ti: 0, tm=nc)
            r3 = lambda a, rows: a.reshape(b, rows, a.shape[-1])
            r4 = lambda a: a.reshape(b, a.shape[0] // b, a.shape[1], a.shape[2])
            oa = _window_gqa(r3(qa, n), r3(ka, n), r3(va, n), r3(kca, nc), r3(vca, nc), attn_sink[i], tq=tq_a)
            pending = [(fam, lj) for lj in range(l, depth) for fam in (0, 1) if (lj, fam) > (l, 0)]
            ob, cast = _mla_attention(r3(qb, n), r3(kb, n), r4(vbt), r3(kcb, nc), r4(vcbt),
                                      [(w, lj) for fam, lj in pending for w in ffn_f32[fam]], tq=tq_b)
            for k, key in enumerate(pending):
                ffn_bf16[key] = tuple(cast[3 * k:3 * k + 3])
            xs = _outproj(xs, oa.reshape(t, -1), ob.reshape(t, -1), attn_w_out[i].astype(BF16), m, x_row, tm=tm)
            if ctx_out:
                raise NotImplementedError("context-stream attention output is not needed at this depth")
        else:
            xs = _pool_mixer(xs.reshape(b, n, d), m, gm, pool_w[i].astype(BF16), pool_scale[i].reshape(1, d),
                             tm=tm).reshape(t, d)
            if ctx_out:
                raise NotImplementedError("context-stream pooling output is not needed at this depth")

        w2 = ffn_weights(1, l)
        xs = _ffn(xs, m, 6, ffn_row, g2, *w2, fn, tm=tm_ffn, tf=tf, final_norm=(l == depth - 1))
        if ctx_out:
            cx = _ffn(cx, m, 6, ctx_row, g2, *w2, fn, tm=tmc, tf=tf, final_norm=False)
    return xs.reshape(b, n, d)
```

```python
import functools

import jax
import jax.numpy as jnp
from jax import lax
from jax.experimental import pallas as pl
from jax.experimental.pallas import tpu as pltpu

F32 = jnp.float32
BF16 = jnp.bfloat16

LANES = 128
SUBLANES = 8
BF16_SUBLANES = 16
ROW_CHUNK = BF16_SUBLANES
NORM_COL_BLOCK = 4 * LANES
MIB = 1 << 20

EPS = 1e-6
ROPE_BASE = 10000.0
GRID_W = 64
N_MOD = 9
A_HEADS = 8
A_KV_HEADS = 2
A_REP = A_HEADS // A_KV_HEADS
A_HEAD_DIM = 128
A_WINDOW = 128
B_HEADS = 8
B_NOPE = 128
B_ROPE = 64
B_QK = B_NOPE + B_ROPE
B_V = 128
B_Q_RANK = 768
B_KV_RANK = 256
B_PAD = 2 * LANES
B_VA = B_V + BF16_SUBLANES
A_Q_W = A_HEADS * A_HEAD_DIM
A_KV_W = A_KV_HEADS * A_HEAD_DIM
OFF_AK = A_Q_W
OFF_AV = OFF_AK + A_KV_W
OFF_CQ = OFF_AV + A_KV_W
OFF_CKV = OFF_CQ + B_Q_RANK
OFF_KR = OFF_CKV + B_KV_RANK
IN_W = OFF_KR + B_ROPE
IN_W_PAD = OFF_KR + LANES
POOL_WINDOWS = (2, 4, 8, 16)
POOL_HALO = SUBLANES
NEG_BIG = -1e30
LOG2_E = 1.4426950408889634
MLA_GROUP = 4

MOD_ROWS = 8


def _params(semantics, vmem_mib):
    return pltpu.CompilerParams(dimension_semantics=semantics, vmem_limit_bytes=vmem_mib * MIB)


def _const_spec(shape):
    nd = len(shape)
    return pl.BlockSpec(shape, lambda *_: (0,) * nd, pipeline_mode=pl.Buffered(1))


def _rms_mod(x, g, shift, scale):
    y = x * lax.rsqrt(jnp.mean(x * x, axis=-1, keepdims=True) + EPS)
    return (y * g) * (1.0 + scale) + shift


def _rms_mod_rows(x_ref, g_ref, sh_ref, sc_ref, r_ref, h_ref, copy_ref=None):
    rows, d = x_ref.shape
    n_chunks = rows // ROW_CHUNK
    unroll = min(8, n_chunks)
    rows_of = lambda i: pl.ds(pl.multiple_of(i * ROW_CHUNK, ROW_CHUNK), ROW_CHUNK)

    def partial_sums(i, carry):
        x = x_ref[rows_of(i), :]
        acc = x[:, 0:LANES] * x[:, 0:LANES]
        for k in range(1, d // LANES):
            xk = x[:, k * LANES:(k + 1) * LANES]
            acc = acc + xk * xk
        r_ref[rows_of(i), :] = acc
        return carry

    lax.fori_loop(0, n_chunks, partial_sums, 0, unroll=unroll)
    mean_sq = jnp.sum(r_ref[...], axis=-1, keepdims=True) * (1.0 / d)
    r_ref[...] = jnp.broadcast_to(lax.rsqrt(mean_sq + EPS), r_ref.shape)

    gain = g_ref[...] * (1.0 + sc_ref[0])
    shift = sh_ref[0]
    cb = min(d, NORM_COL_BLOCK)
    for c0 in range(0, d, cb):
        cols = slice(c0, c0 + cb)
        gain_c, shift_c = gain[:, cols], shift[:, cols]

        def apply(i, carry, cols=cols, gain_c=gain_c, shift_c=shift_c):
            x = x_ref[rows_of(i), cols]
            r = jnp.concatenate([r_ref[rows_of(i), :]] * (cb // LANES), axis=1)
            h_ref[rows_of(i), cols] = ((x * r) * gain_c + shift_c).astype(BF16)
            if copy_ref is not None:
                copy_ref[rows_of(i), cols] = x
            return carry

        lax.fori_loop(0, n_chunks, apply, 0, unroll=unroll)


def _dot(a, b):
    return jnp.dot(a, b, preferred_element_type=F32)


def _dot_nt(a, b):
    return lax.dot_general(a, b, (((1,), (1,)), ((), ())), preferred_element_type=F32)


def _dot_tn(a, b):
    return lax.dot_general(a, b, (((0,), (0,)), ((), ())), preferred_element_type=F32)


def _mods_kernel(c_ref, w_ref, b_ref, o_ref):
    c = c_ref[...]
    a = (c * jax.nn.sigmoid(c)).astype(BF16)
    o_ref[0] = _dot(a, w_ref[0].astype(BF16)) + b_ref[0]


def _mods(c, c_ctx, w_ada, b_ada):
    depth, d, nd = w_ada.shape
    b = c.shape[0]
    assert b + 1 <= MOD_ROWS
    cp = jnp.concatenate([c, c_ctx[None, :], jnp.zeros((MOD_ROWS - b - 1, d), F32)], axis=0)
    tn = min(d, 1024)
    assert nd % tn == 0
    out = pl.pallas_call(
        _mods_kernel,
        grid=(depth, nd // tn),
        in_specs=[
            pl.BlockSpec((MOD_ROWS, d), lambda l, j: (0, 0)),
            pl.BlockSpec((1, d, tn), lambda l, j: (l, 0, j)),
            pl.BlockSpec((1, 1, tn), lambda l, j: (l, 0, j)),
        ],
        out_specs=pl.BlockSpec((1, MOD_ROWS, tn), lambda l, j: (l, 0, j)),
        out_shape=jax.ShapeDtypeStruct((depth, MOD_ROWS, nd), F32),
        compiler_params=_params(("arbitrary", "arbitrary"), 40),
        name="adaln_mods",
    )(cp, w_ada, b_ada.reshape(depth, 1, nd))
    return out.reshape(depth, MOD_ROWS * N_MOD, 1, d)


def _mod_spec(d, row_of_tile, k):
    return pl.BlockSpec((1, 1, d), lambda i, *_: (row_of_tile(i) * N_MOD + k, 0, 0))


def _ffn_kernel(x_ref, g_ref, sh_ref, sc_ref, gt_ref, wg_ref, wu_ref, wd_ref, fn_ref,
                o_ref, h_ref, r_ref, *, final_norm):
    j = pl.program_id(1)

    def accumulate(base_ref):
        h = h_ref[...]
        gate = _dot(h, wg_ref[...])
        up = _dot(h, wu_ref[...])
        a = (gate * jax.nn.sigmoid(gate)) * up
        o_ref[...] = base_ref[...] + (0.5 * gt_ref[0]) * _dot(a.astype(BF16), wd_ref[...])

    @pl.when(j == 0)
    def _():
        _rms_mod_rows(x_ref, g_ref, sh_ref, sc_ref, r_ref, h_ref)
        accumulate(x_ref)

    @pl.when(j > 0)
    def _():
        accumulate(o_ref)

    if final_norm:
        @pl.when(j == pl.num_programs(1) - 1)
        def _():
            out = o_ref[...]
            o_ref[...] = out * lax.rsqrt(jnp.mean(out * out, axis=-1, keepdims=True) + EPS) * fn_ref[...]


def _ffn(x, mods, mod_base, row_of_tile, g, wg, wu, wd, fn, *, tm, tf, final_norm):
    t, d = x.shape
    f = wg.shape[1]
    return pl.pallas_call(
        functools.partial(_ffn_kernel, final_norm=final_norm),
        grid=(t // tm, f // tf),
        in_specs=[
            pl.BlockSpec((tm, d), lambda i, j: (i, 0)),
            pl.BlockSpec((1, d), lambda i, j: (0, 0)),
            _mod_spec(d, row_of_tile, mod_base),
            _mod_spec(d, row_of_tile, mod_base + 1),
            _mod_spec(d, row_of_tile, mod_base + 2),
            pl.BlockSpec((d, tf), lambda i, j: (0, j)),
            pl.BlockSpec((d, tf), lambda i, j: (0, j)),
            pl.BlockSpec((tf, d), lambda i, j: (j, 0)),
            pl.BlockSpec((1, d), lambda i, j: (0, 0)),
        ],
        out_specs=pl.BlockSpec((tm, d), lambda i, j: (i, 0)),
        out_shape=jax.ShapeDtypeStruct((t, d), F32),
        scratch_shapes=[pltpu.VMEM((tm, d), BF16), pltpu.VMEM((tm, LANES), F32)],
        compiler_params=_params(("parallel", "arbitrary"), 63),
        name="swiglu_half_step",
    )(x, g, mods, mods, mods, wg, wu, wd, fn)


def _rope_a(x, cos, sin):
    return x * cos + pltpu.roll(x, A_HEAD_DIM // 2, 1) * sin


def _rope_b(x, cos, sin_lo, sin_hi):
    half = B_ROPE // 2
    return x * cos + pltpu.roll(x, LANES - half, 1) * sin_lo + pltpu.roll(x, half, 1) * sin_hi


def _inproj_kernel(x_ref, g_ref, sh_ref, sc_ref, win_ref, qn_ref, wuq_ref, kvn_ref, wuk_ref, wuvt_ref,
                   cosa_ref, sina_ref, cosb_ref, sinlo_ref, sinhi_ref,
                   qa_ref, ka_ref, va_ref, qb_ref, kb_ref, vbt_ref):
    h = _rms_mod(x_ref[...], g_ref[...], sh_ref[0], sc_ref[0]).astype(BF16)
    cosa, sina = cosa_ref[...], sina_ref[...]
    cosb, sinlo, sinhi = cosb_ref[...], sinlo_ref[...], sinhi_ref[...]

    aq = _dot(h, win_ref[:, 0:OFF_AK]) * (A_HEAD_DIM ** -0.5 * LOG2_E)
    for s in range(A_HEADS):
        sl = slice(s * LANES, (s + 1) * LANES)
        qa_ref[:, sl] = _rope_a(aq[:, sl], cosa, sina).astype(BF16)
    akv = _dot(h, win_ref[:, OFF_AK:OFF_CQ])
    for s in range(A_KV_HEADS):
        sl = slice(s * LANES, (s + 1) * LANES)
        ka_ref[:, sl] = _rope_a(akv[:, sl], cosa, sina).astype(BF16)
    va_ref[...] = akv[:, A_KV_W:].astype(BF16)

    cq = _dot(h, win_ref[:, OFF_CQ:OFF_CKV])
    cqn = (cq * lax.rsqrt(jnp.mean(cq * cq, axis=-1, keepdims=True) + EPS) * qn_ref[...]).astype(BF16)
    qb =_dot(cqn, wuq_ref[...]) * (B_QK ** -0.5 * LOG2_E)
    for hd in range(B_HEADS):
        lo = hd * B_PAD
        qb_ref[:, lo:lo + LANES] = qb[:, lo:lo + LANES].astype(BF16)
        qb_ref[:, lo + LANES:lo + B_PAD] = _rope_b(qb[:, lo + LANES:lo + B_PAD], cosb, sinlo, sinhi).astype(BF16)

    ckr = _dot(h, win_ref[:, OFF_CKV:IN_W_PAD])
    ckv = ckr[:, 0:B_KV_RANK]
    ckvn = (ckv * lax.rsqrt(jnp.mean(ckv * ckv, axis=-1, keepdims=True) + EPS) * kvn_ref[...]).astype(BF16)
    kr = _rope_b(ckr[:, B_KV_RANK:], cosb, sinlo, sinhi).astype(BF16)
    kn = _dot(ckvn, wuk_ref[...])
    for hd in range(B_HEADS):
        lo = hd * B_PAD
        kb_ref[:, lo:lo + LANES] = kn[:, hd * B_NOPE:(hd + 1) * B_NOPE].astype(BF16)
        kb_ref[:, lo + LANES:lo + B_PAD] = kr
    vt = _dot_nt(wuvt_ref[...], ckvn).astype(BF16)
    ones_row = (lax.broadcasted_iota(jnp.int32, (B_VA - B_V, vt.shape[1]), 0) == 0).astype(F32).astype(BF16)
    for hd in range(B_HEADS):
        vbt_ref[0, hd * B_VA:hd * B_VA + B_V, :] = vt[hd * B_V:(hd + 1) * B_V]
        vbt_ref[0, hd * B_VA + B_V:(hd + 1) * B_VA, :] = ones_row


def _inproj(x, mods, row_of_tile, g, win, qn, wuq, kvn, wuk, wuvt, tables, table_tile, *, tm):
    t, d = x.shape
    tok = lambda w: pl.BlockSpec((tm, w), lambda i: (i, 0))
    tab = pl.BlockSpec((tm, LANES), lambda i: (table_tile(i), 0))
    widths = (A_Q_W, A_KV_W, A_KV_W, B_HEADS * B_PAD, B_HEADS * B_PAD)
    vt_rows = B_HEADS * B_VA
    return pl.pallas_call(
        _inproj_kernel,
        grid=(t // tm,),
        in_specs=[
            tok(d),
            _const_spec((1, d)),
            _mod_spec(d, row_of_tile, 3),
            _mod_spec(d, row_of_tile, 4),
            _const_spec(win.shape),
            _const_spec(qn.shape),
            _const_spec(wuq.shape),
            _const_spec(kvn.shape),
            _const_spec(wuk.shape),
            _const_spec(wuvt.shape),
            tab, tab, tab, tab, tab,
        ],
        out_specs=[tok(w) for w in widths] + [pl.BlockSpec((1, vt_rows, tm), lambda i: (i, 0, 0))],
        out_shape=[jax.ShapeDtypeStruct((t, w), BF16) for w in widths]
        + [jax.ShapeDtypeStruct((t // tm, vt_rows, tm), BF16)],
        compiler_params=_params(("parallel",), 56),
        name="attn_in_proj",
    )(x, g, mods, mods, win, qn, wuq, kvn, wuk, wuvt, *tables)


def _window_kernel(sink_ref, q_ref, k_ref, v_ref, kc_ref, vc_ref, o_ref, *, tq, n):
    qi = pl.program_id(1)
    span = tq + 2 * A_WINDOW
    start = pl.multiple_of(jnp.clip(qi * tq - A_WINDOW, 0, n - span), A_WINDOW)
    kpos = start + lax.broadcasted_iota(jnp.int32, (span, tq), 0)
    qpos = qi * tq + lax.broadcasted_iota(jnp.int32, (span, tq), 1)
    bias = jnp.where(jnp.abs(kpos - qpos) <= A_WINDOW, 0.0, NEG_BIG)
    for g in range(A_KV_HEADS):
        gl = slice(g * A_HEAD_DIM, (g + 1) * A_HEAD_DIM)
        heads = [slice((g * A_REP + r) * A_HEAD_DIM, (g * A_REP + r + 1) * A_HEAD_DIM) for r in range(A_REP)]
        k = k_ref[0, pl.ds(start, span), gl]
        v = v_ref[0, pl.ds(start, span), gl]
        q = jnp.concatenate([q_ref[0, :, hl] for hl in heads], axis=0)
        s_loc = _dot_nt(k, q)
        s_loc = jnp.concatenate([s_loc[:, r * tq:(r + 1) * tq] + bias for r in range(A_REP)], axis=1)
        s_ctx = _dot_nt(kc_ref[0, :, gl], q)
        sink = jnp.concatenate([jnp.full((1, tq), sink_ref[g * A_REP + r] * LOG2_E, F32) for r in range(A_REP)],
                               axis=1)
        m = jnp.maximum(jnp.maximum(jnp.max(s_loc, axis=0, keepdims=True),
                                    jnp.max(s_ctx, axis=0, keepdims=True)), sink)
        p_loc = jnp.exp2(s_loc - m)
        p_ctx = jnp.exp2(s_ctx - m)
        denom = (jnp.sum(p_loc, axis=0, keepdims=True) + jnp.sum(p_ctx, axis=0, keepdims=True)
                 + jnp.exp2(sink - m))
        o_t = _dot_tn(v, p_loc.astype(BF16)) + _dot_tn(vc_ref[0, :, gl], p_ctx.astype(BF16))
        o = jnp.transpose(o_t / denom)
        for r, hl in enumerate(heads):
            o_ref[0, :, hl] = o[r * tq:(r + 1) * tq].astype(BF16)


def _window_gqa(qa, ka, va, kca, vca, sink, *, tq):
    b, n, _ = qa.shape
    nc = kca.shape[1]
    assert n >= tq + 2 * A_WINDOW
    return pl.pallas_call(
        functools.partial(_window_kernel, tq=tq, n=n),
        grid=(b, n // tq),
        in_specs=[
            pl.BlockSpec(memory_space=pltpu.SMEM),
            pl.BlockSpec((1, tq, A_Q_W), lambda bi, qi: (bi, qi, 0)),
            pl.BlockSpec((1, n, A_KV_W), lambda bi, qi: (bi, 0, 0)),
            pl.BlockSpec((1, n, A_KV_W), lambda bi, qi: (bi, 0, 0)),
            pl.BlockSpec((1, nc, A_KV_W), lambda bi, qi: (bi, 0, 0)),
            pl.BlockSpec((1, nc, A_KV_W), lambda bi, qi: (bi, 0, 0)),
        ],
        out_specs=pl.BlockSpec((1, tq, A_Q_W), lambda bi, qi: (bi, qi, 0)),
        out_shape=jax.ShapeDtypeStruct((b, n, A_Q_W), BF16),
        compiler_params=_params(("parallel", "arbitrary"), 48),
        name="window_gqa",
    )(sink, qa, ka, va, kca, vca)


def _mla_kernel(q_ref, k_ref, vt_ref, kc_ref, vct_ref, *rest, tk, n_cast):
    cast_src, o_ref, cast_dst = rest[:n_cast], rest[n_cast], rest[n_cast + 1:2 * n_cast + 1]
    sa_ref, sb_ref, qt_ref = rest[2 * n_cast + 1:]
    for src, dst in zip(cast_src, cast_dst):
        dst[...] = src[...].astype(BF16)

    qt_ref[...] = q_ref[0].T
    qt = qt_ref[...]
    nchunks = vt_ref.shape[1]

    def scores(c):
        return _dot(k_ref[0, pl.ds(pl.multiple_of(c * tk, tk), tk), :], qt)

    def probs(s, m):
        return jnp.exp2(s - m).astype(BF16)

    def update(s, vt, carry):
        m_old, acc_old = carry
        m = jnp.maximum(m_old, jnp.max(s, axis=0, keepdims=True))
        acc = jnp.exp2(m_old - m) * acc_old + _dot(vt, probs(s, m))
        return m, acc

    sa_ref[...] = scores(0)
    s = _dot(kc_ref[0], qt)
    m = jnp.max(s, axis=0, keepdims=True)
    carry = (m, _dot(vct_ref[0, 0], probs(s, m)))

    bufs = (sa_ref, sb_ref)

    def group(c0, carry, last):
        for j in range(MLA_GROUP):
            if not (last and j == MLA_GROUP - 1):
                bufs[(j + 1) % 2][...] = scores(c0 + j + 1)
            carry = update(bufs[j % 2][...], vt_ref[0, c0 + j], carry)
        return carry

    carry = lax.fori_loop(0, nchunks // MLA_GROUP - 1,
                          lambda i, cr: group(i * MLA_GROUP, cr, False), carry)
    _, acc = group(nchunks - MLA_GROUP, carry, True)
    o_ref[0] = jnp.transpose(acc[0:B_V] / acc[B_V:B_V + 1]).astype(BF16)


def _cast_split(rows, cols, steps):
    for cb in range(1, steps + 1):
        rb = steps // cb
        if (steps % cb == 0 and rows % rb == 0 and cols % cb == 0
                and (rows // rb) % BF16_SUBLANES == 0 and (cols // cb) % LANES == 0):
            return rb, cb
    raise ValueError(f"cannot split a ({rows}, {cols}) matrix into {steps} aligned blocks")


def _mla_attention(qb, kb, vbt, kcb, vcbt, casts, *, tq):
    b, n, _ = qb.shape
    nc = kcb.shape[1]
    nchunks, _, tk = vbt.shape[1:]
    nq = n // tq
    assert nchunks * tk == n and nchunks % MLA_GROUP == 0 and vcbt.shape[1] == 1 and vcbt.shape[3] == nc
    steps = b * B_HEADS * nq

    cast_in, cast_out, cast_shapes = [], [], []
    for w, layer in casts:
        _, rows, cols = w.shape
        rb, cb = _cast_split(rows, cols, steps)
        blk = (rows // rb, cols // cb)
        step = lambda bi, hi, qi: (bi * B_HEADS + hi) * nq + qi
        cast_in.append(pl.BlockSpec((None,) + blk,
                                    lambda bi, hi, qi, layer=layer, cb=cb: (layer, step(bi, hi, qi) // cb,
                                                                            step(bi, hi, qi) % cb)))
        cast_out.append(pl.BlockSpec(blk, lambda bi, hi, qi, cb=cb: (step(bi, hi, qi) // cb, step(bi, hi, qi) % cb)))
        cast_shapes.append(jax.ShapeDtypeStruct((rows, cols), BF16))

    out = pl.pallas_call(
        functools.partial(_mla_kernel, tk=tk, n_cast=len(casts)),
        grid=(b, B_HEADS, nq),
        in_specs=[
            pl.BlockSpec((1, tq, B_PAD), lambda bi, hi, qi: (bi, qi, hi)),
            pl.BlockSpec((1, n, B_PAD), lambda bi, hi, qi: (bi, 0, hi)),
            pl.BlockSpec((1, nchunks, B_VA, tk), lambda bi, hi, qi: (bi, 0, hi, 0)),
            pl.BlockSpec((1, nc, B_PAD), lambda bi, hi, qi: (bi, 0, hi)),
            pl.BlockSpec((1, 1, B_VA, nc), lambda bi, hi, qi: (bi, 0, hi, 0)),
        ] + cast_in,
        out_specs=[pl.BlockSpec((1, tq, B_V), lambda bi, hi, qi: (bi, qi, hi))] + cast_out,
        out_shape=[jax.ShapeDtypeStruct((b, n, B_HEADS * B_V), BF16)] + cast_shapes,
        scratch_shapes=[pltpu.VMEM((tk, tq), F32), pltpu.VMEM((tk, tq), F32), pltpu.VMEM((B_PAD, tq), BF16)],
        compiler_params=_params(("arbitrary", "arbitrary", "arbitrary"), 48),
        name="mla_attention",
    )(qb, kb, vbt, kcb, vcbt, *[w for w, _ in casts])
    return out[0], out[1:]


def _outproj_kernel(x_ref, oa_ref, ob_ref, w_ref, gt_ref, o_ref):
    ka = oa_ref.shape[1]
    y = _dot(oa_ref[...], w_ref[0:ka, :]) + _dot(ob_ref[...], w_ref[ka:, :])
    o_ref[...] = x_ref[...] + gt_ref[0] * y


def _outproj(x, oa, ob, w, mods, row_of_tile, *, tm):
    t, d = x.shape
    return pl.pallas_call(
        _outproj_kernel,
        grid=(t // tm,),
        in_specs=[
            pl.BlockSpec((tm, d), lambda i: (i, 0)),
            pl.BlockSpec((tm, oa.shape[1]), lambda i: (i, 0)),
            pl.BlockSpec((tm, ob.shape[1]), lambda i: (i, 0)),
            _const_spec(w.shape),
            _mod_spec(d, row_of_tile, 5),
        ],
        out_specs=pl.BlockSpec((tm, d), lambda i: (i, 0)),
        out_shape=jax.ShapeDtypeStruct((t, d), F32),
        compiler_params=_params(("parallel",), 48),
        name="attn_out_proj",
    )(x, oa, ob, w, mods)


def _pool_kernel(x_ref, xp_ref, xn_ref, g_ref, sh_ref, sc_ref, gt_ref, w_ref, ps_ref, o_ref, hp_ref,
                 *, tm, n):
    i = pl.program_id(1)
    last = pl.num_programs(1) - 1
    g, sh, sc = g_ref[...], sh_ref[0], sc_ref[0]
    x = x_ref[0]
    hp_ref[0:POOL_HALO, :] = jnp.where(i > 0, _rms_mod(xp_ref[0], g, sh, sc), 0.0)
    hp_ref[POOL_HALO:POOL_HALO + tm, :] = _rms_mod(x, g, sh, sc)
    hp_ref[POOL_HALO + tm:, :] = jnp.where(i < last, _rms_mod(xn_ref[0], g, sh, sc), 0.0)

    t = i * tm + lax.broadcasted_iota(jnp.int32, (tm, 1), 0)
    gw = x.shape[1] // len(POOL_WINDOWS)
    for gi, w in enumerate(POOL_WINDOWS):
        cols = slice(gi * gw, (gi + 1) * gw)
        rows = tm + 2 * POOL_HALO
        shift_up = lambda a, k: a if k == 0 else pltpu.roll(a, rows - k, 0)
        tot = shift_up(hp_ref[:, cols], POOL_HALO - w // 2)
        k = 1
        while k < w:
            tot = tot + shift_up(tot, k)
            k *= 2
        tot = tot[0:tm]
        cnt =(jnp.minimum(t - w // 2 + w, n) - jnp.maximum(t - w // 2, 0)).astype(F32)
        diff = tot / cnt - hp_ref[pl.ds(POOL_HALO, tm), cols]
        y = _dot(diff.astype(BF16), w_ref[gi]) * ps_ref[:, cols]
        o_ref[0, :, cols] = x[:, cols] + gt_ref[0][:, cols] * y


def _pool_mixer(x3, mods, g, w, ps, *, tm):
    b, n, d = x3.shape
    hb = tm // POOL_HALO
    mod = lambda k: pl.BlockSpec((1, 1, d), lambda bi, i: (bi * N_MOD + k, 0, 0))
    return pl.pallas_call(
        functools.partial(_pool_kernel, tm=tm, n=n),
        grid=(b, n // tm),
        in_specs=[
            pl.BlockSpec((1, tm, d), lambda bi, i: (bi, i, 0)),
            pl.BlockSpec((1, POOL_HALO, d), lambda bi, i: (bi, jnp.maximum(i * hb - 1, 0), 0)),
            pl.BlockSpec((1, POOL_HALO, d), lambda bi, i: (bi, jnp.minimum((i + 1) * hb, n // POOL_HALO - 1), 0)),
            pl.BlockSpec((1, d), lambda bi, i: (0, 0)),
            mod(3), mod(4), mod(5),
            pl.BlockSpec(w.shape, lambda bi, i: (0, 0, 0)),
            pl.BlockSpec((1, d), lambda bi, i: (0, 0)),
        ],
        out_specs=pl.BlockSpec((1, tm, d), lambda bi, i: (bi, i, 0)),
        out_shape=jax.ShapeDtypeStruct((b, n, d), F32),
        scratch_shapes=[pltpu.VMEM((tm + 2 * POOL_HALO, d), F32)],
        compiler_params=_params(("parallel", "arbitrary"), 48),
        name="pool_mixer",
    )(x3, x3, x3, g, mods, mods, mods, w, ps)


def _rope_tables(n):
    rows = n // GRID_W
    row = jnp.arange(rows, dtype=F32)
    col = jnp.arange(GRID_W, dtype=F32)

    def cos_sin(rot_dim):
        nf = rot_dim // 4
        inv = ROPE_BASE ** (-jnp.arange(nf, dtype=F32) / nf)
        expand = lambda fn: jnp.concatenate([jnp.repeat(fn(row[:, None] * inv), GRID_W, axis=0),
                                             jnp.tile(fn(col[:, None] * inv), (rows, 1))], axis=-1)
        return expand(jnp.cos), expand(jnp.sin)

    ca, sa = cos_sin(A_HEAD_DIM)
    cb, sb = cos_sin(B_ROPE)
    zb = jnp.zeros_like(cb)
    zpad = jnp.zeros((n, LANES - B_ROPE), F32)
    return (jnp.concatenate([ca, ca], axis=-1), jnp.concatenate([-sa, sa], axis=-1),
            jnp.concatenate([cb, cb, zpad], axis=-1), jnp.concatenate([-sb, zb, zpad], axis=-1),
            jnp.concatenate([zb, sb, zpad], axis=-1))


def _identity_tables(n):
    one = jnp.ones((n, LANES), F32)
    zero = jnp.zeros((n, LANES), F32)
    return (one, zero, one, zero, zero)


def _attn_weights(w_in, w_uq, w_ukv):
    d = w_in.shape[0]
    win = jnp.concatenate([w_in, jnp.zeros((d, IN_W_PAD - IN_W), w_in.dtype)], axis=1).astype(BF16)
    uq = w_uq.reshape(B_Q_RANK, B_HEADS, B_QK)
    uq = jnp.concatenate([uq, jnp.zeros((B_Q_RANK, B_HEADS, B_PAD - B_QK), uq.dtype)], axis=-1)
    ukv = w_ukv.reshape(B_KV_RANK, B_HEADS, B_NOPE + B_V)
    uk = ukv[:, :, :B_NOPE].reshape(B_KV_RANK, B_HEADS * B_NOPE)
    uvt = ukv[:, :, B_NOPE:].reshape(B_KV_RANK, B_HEADS * B_V).T
    return win, uq.reshape(B_Q_RANK, B_HEADS * B_PAD).astype(BF16), uk.astype(BF16), uvt.astype(BF16)


def kernel(x, c, ctx, c_ctx, w_ada, b_ada, norm_ffn1, norm_mix, norm_ffn2, ffn1_w_gate, ffn1_w_up, ffn1_w_down,
           ffn2_w_gate, ffn2_w_up, ffn2_w_down, attn_w_in, attn_sink, mla_q_norm, mla_w_uq, mla_kv_norm, mla_w_ukv,
           attn_w_out, pool_w, pool_scale, final_norm):
    b, n, d = x.shape
    nc = ctx.shape[1]
    depth = w_ada.shape[0]
    f = ffn1_w_gate.shape[2]
    t = b * n

    tm = min(512, n)
    tm_ffn = min(1024, n)
    tf = min(512, f)
    tmc = min(512, b * nc)
    tq_a = min(256, n)
    tq_b = min(1024, n)
    assert n % tm == 0 and f % tf == 0 and (b * nc) % tmc == 0 and n % GRID_W == 0

    tiles_per_sample = n // tm
    x_row = lambda i: i // tiles_per_sample
    ffn_row = lambda i: i // (n // tm_ffn)
    ctx_row = lambda i: b

    mods = _mods(c, c_ctx, w_ada, b_ada)
    rope = _rope_tables(n)
    fn = final_norm.reshape(1, d)

    ffn_f32 = ((ffn1_w_gate, ffn1_w_up, ffn1_w_down), (ffn2_w_gate, ffn2_w_up, ffn2_w_down))
    ffn_bf16 = {}

    def ffn_weights(family, layer):
        if (family, layer) not in ffn_bf16:
            ffn_bf16[(family, layer)] = tuple(w[layer].astype(BF16) for w in ffn_f32[family])
        return ffn_bf16[(family, layer)]

    xs = x.reshape(t, d)
    cx = ctx.reshape(b * nc, d)
    for l in range(depth):
        is_attn = l % 2 == 0
        ctx_out = any(j % 2 == 0 for j in range(l + 1, depth))
        ctx_in = is_attn or ctx_out
        i = l // 2
        m = mods[l]
        g1, gm, g2 = (v[l].reshape(1, d) for v in (norm_ffn1, norm_mix, norm_ffn2))
        w1 = ffn_weights(0, l)

        xs = _ffn(xs, m, 0, ffn_row, g1, *w1, fn, tm=tm_ffn, tf=tf, final_norm=False)
        if ctx_in:
            cx = _ffn(cx, m, 0, ctx_row, g1, *w1, fn, tm=tmc, tf=tf, final_norm=False)

        if is_attn:
            win, wuq, wuk, wuvt = _attn_weights(attn_w_in[i], mla_w_uq[i], mla_w_ukv[i])
            qn = mla_q_norm[i].reshape(1, -1)
            kvn = mla_kv_norm[i].reshape(1, -1)
            qa, ka, va, qb, kb, vbt = _inproj(xs, m, x_row, gm, win, qn, wuq, kvn, wuk, wuvt, rope,
                                              lambda ti: ti % tiles_per_sample, tm=tm)
            _, kca, vca, _, kcb, vcbt = _inproj(cx, m, ctx_row, gm, win, qn, wuq, kvn, wuk, wuvt,
                                                _identity_tables(nc), lambda ti: 0, tm=nc)
            r3 = lambda a, rows: a.reshape(b, rows, a.shape[-1])
            r4 = lambda a: a.reshape(b, a.shape[0] // b, a.shape[1], a.shape[2])
            oa = _window_gqa(r3(qa, n), r3(ka, n), r3(va, n), r3(kca, nc), r3(vca, nc), attn_sink[i], tq=tq_a)
            pending = [(fam, lj) for lj in range(l, depth) for fam in (0, 1) if (lj, fam) > (l, 0)]
            ob, cast = _mla_attention(r3(qb, n), r3(kb, n), r4(vbt), r3(kcb, nc), r4(vcbt),
                                      [(w, lj) for fam, lj in pending for w in ffn_f32[fam]], tq=tq_b)
            for k, key in enumerate(pending):
                ffn_bf16[key] = tuple(cast[3 * k:3 * k + 3])
            xs = _outproj(xs, oa.reshape(t, -1), ob.reshape(t, -1), attn_w_out[i].astype(BF16), m, x_row, tm=tm)
            if ctx_out:
                raise NotImplementedError("context-stream attention output is not needed at this depth")
        else:
            xs = _pool_mixer(xs.reshape(b, n, d), m, gm, pool_w[i].astype(BF16), pool_scale[i].reshape(1, d),
                             tm=tm).reshape(t, d)
            if ctx_out:
                raise NotImplementedError("context-stream pooling output is not needed at this depth")

        w2 = ffn_weights(1, l)
        xs = _ffn(xs, m, 6, ffn_row, g2, *w2, fn, tm=tm_ffn, tf=tf, final_norm=(l == depth - 1))
        if ctx_out:
            cx = _ffn(cx, m, 6, ctx_row, g2, *w2, fn, tm=tmc, tf=tf, final_norm=False)
    return xs.reshape(b, n, d)
```

```python
import functools

import jax
import jax.numpy as jnp
from jax import lax
from jax.experimental import pallas as pl
from jax.experimental.pallas import tpu as pltpu

F32 = jnp.float32
BF16 = jnp.bfloat16

LANES = 128
SUBLANES = 8
BF16_SUBLANES = 16
ROW_CHUNK = BF16_SUBLANES
NORM_COL_BLOCK = 4 * LANES
MIB = 1 << 20

EPS = 1e-6
ROPE_BASE = 10000.0
GRID_W = 64
N_MOD = 9
A_HEADS = 8
A_KV_HEADS = 2
A_REP = A_HEADS // A_KV_HEADS
A_HEAD_DIM = 128
A_WINDOW = 128
B_HEADS = 8
B_NOPE = 128
B_ROPE = 64
B_QK = B_NOPE + B_ROPE
B_V = 128
B_Q_RANK = 768
B_KV_RANK = 256
B_PAD = 2 * LANES
B_VA = B_V + BF16_SUBLANES
A_Q_W = A_HEADS * A_HEAD_DIM
A_KV_W = A_KV_HEADS * A_HEAD_DIM
OFF_AK = A_Q_W
OFF_AV = OFF_AK + A_KV_W
OFF_CQ = OFF_AV + A_KV_W
OFF_CKV = OFF_CQ + B_Q_RANK
OFF_KR = OFF_CKV + B_KV_RANK
IN_W = OFF_KR + B_ROPE
IN_W_PAD = OFF_KR + LANES
POOL_WINDOWS = (2, 4, 8, 16)
POOL_HALO = SUBLANES
NEG_BIG = -1e30
LOG2_E = 1.4426950408889634
FFN_FIRST_STEP_ROWS = 256
INPROJ_ROWS = 512
MLA_GROUP = 4

MOD_ROWS = 8


def _params(semantics, vmem_mib):
    return pltpu.CompilerParams(dimension_semantics=semantics, vmem_limit_bytes=vmem_mib * MIB)


def _const_spec(shape):
    nd = len(shape)
    return pl.BlockSpec(shape, lambda *_: (0,) * nd, pipeline_mode=pl.Buffered(1))


def _rms_mod(x, g, shift, scale):
    y = x * lax.rsqrt(jnp.mean(x * x, axis=-1, keepdims=True) + EPS)
    return (y * g) * (1.0 + scale) + shift


def _rms_mod_rows(x_ref, g_ref, sh_ref, sc_ref, r_ref, h_ref, copy_ref=None):
    rows, d = x_ref.shape
    n_chunks = rows // ROW_CHUNK
    unroll = min(8, n_chunks)
    rows_of = lambda i: pl.ds(pl.multiple_of(i * ROW_CHUNK, ROW_CHUNK), ROW_CHUNK)

    def partial_sums(i, carry):
        x = x_ref[rows_of(i), :]
        acc = x[:, 0:LANES] * x[:, 0:LANES]
        for k in range(1, d // LANES):
            xk = x[:, k * LANES:(k + 1) * LANES]
            acc = acc + xk * xk
        r_ref[rows_of(i), :] = acc
        return carry

    lax.fori_loop(0, n_chunks, partial_sums, 0, unroll=unroll)
    mean_sq = jnp.sum(r_ref[...], axis=-1, keepdims=True) * (1.0 / d)
    r_ref[...] = jnp.broadcast_to(lax.rsqrt(mean_sq + EPS), r_ref.shape)

    gain = g_ref[...] * (1.0 + sc_ref[0])
    shift = sh_ref[0]
    cb = min(d, NORM_COL_BLOCK)
    for c0 in range(0, d, cb):
        cols = slice(c0, c0 + cb)
        gain_c, shift_c = gain[:, cols], shift[:, cols]

        def apply(i, carry, cols=cols, gain_c=gain_c, shift_c=shift_c):
            x = x_ref[rows_of(i), cols]
            r = jnp.concatenate([r_ref[rows_of(i), :]] * (cb // LANES), axis=1)
            h_ref[rows_of(i), cols] = ((x * r) * gain_c + shift_c).astype(BF16)
            if copy_ref is not None:
                copy_ref[rows_of(i), cols] = x
            return carry

        lax.fori_loop(0, n_chunks, apply, 0, unroll=unroll)


def _dot(a, b):
    return jnp.dot(a, b, preferred_element_type=F32)


def _dot_nt(a, b):
    return lax.dot_general(a, b, (((1,), (1,)), ((), ())), preferred_element_type=F32)


def _dot_tn(a, b):
    return lax.dot_general(a, b, (((0,), (0,)), ((), ())), preferred_element_type=F32)


def _mods_kernel(c_ref, w_ref, b_ref, o_ref):
    c = c_ref[...]
    a = (c * jax.nn.sigmoid(c)).astype(BF16)
    o_ref[0] = _dot(a, w_ref[0].astype(BF16)) + b_ref[0]


def _mods(c, c_ctx, w_ada, b_ada):
    depth, d, nd = w_ada.shape
    b = c.shape[0]
    assert b + 1 <= MOD_ROWS
    cp = jnp.concatenate([c, c_ctx[None, :], jnp.zeros((MOD_ROWS - b - 1, d), F32)], axis=0)
    tn = min(d, 1024)
    assert nd % tn == 0
    out = pl.pallas_call(
        _mods_kernel,
        grid=(depth, nd // tn),
        in_specs=[
            pl.BlockSpec((MOD_ROWS, d), lambda l, j: (0, 0)),
            pl.BlockSpec((1, d, tn), lambda l, j: (l, 0, j)),
            pl.BlockSpec((1, 1, tn), lambda l, j: (l, 0, j)),
        ],
        out_specs=pl.BlockSpec((1, MOD_ROWS, tn), lambda l, j: (l, 0, j)),
        out_shape=jax.ShapeDtypeStruct((depth, MOD_ROWS, nd), F32),
        compiler_params=_params(("arbitrary", "arbitrary"), 40),
        name="adaln_mods",
    )(cp, w_ada, b_ada.reshape(depth, 1, nd))
    return out.reshape(depth, MOD_ROWS * N_MOD, 1, d)


def _mod_spec(d, row_of_tile, k):
    return pl.BlockSpec((1, 1, d), lambda i, *_: (row_of_tile(i) * N_MOD + k, 0, 0))


def _ffn_kernel(x_ref, g_ref, sh_ref, sc_ref, gt_ref, wg_ref, wu_ref, wd_ref, fn_ref,
                o_ref, h_ref, *, final_norm):
    j = pl.program_id(1)

    def accumulate(h, base):
        gate = _dot(h, wg_ref[...])
        up = _dot(h, wu_ref[...])
        a = (gate * jax.nn.sigmoid(gate)) * up
        return base + (0.5 * gt_ref[0]) * _dot(a.astype(BF16), wd_ref[...])

    @pl.when(j == 0)
    def _():
        gain = g_ref[...] * (1.0 + sc_ref[0])
        shift = sh_ref[0]
        rb = min(FFN_FIRST_STEP_ROWS, x_ref.shape[0])
        for r0 in range(0, x_ref.shape[0], rb):
            rows = slice(r0, r0 + rb)
            x = x_ref[rows, :]
            h = ((x * lax.rsqrt(jnp.mean(x * x, axis=-1, keepdims=True) + EPS)) * gain + shift).astype(BF16)
            h_ref[rows, :] = h
            o_ref[rows, :] = accumulate(h, x_ref[rows, :])

    @pl.when(j > 0)
    def _():
        o_ref[...] = accumulate(h_ref[...], o_ref[...])

    if final_norm:
        @pl.when(j == pl.num_programs(1) - 1)
        def _():
            out = o_ref[...]
            o_ref[...] = out * lax.rsqrt(jnp.mean(out * out, axis=-1, keepdims=True) + EPS) * fn_ref[...]


def _ffn(x, mods, mod_base, row_of_tile, g, wg, wu, wd, fn, *, tm, tf, final_norm):
    t, d = x.shape
    f = wg.shape[1]
    return pl.pallas_call(
        functools.partial(_ffn_kernel, final_norm=final_norm),
        grid=(t // tm, f // tf),
        in_specs=[
            pl.BlockSpec((tm, d), lambda i, j: (i, 0)),
            pl.BlockSpec((1, d), lambda i, j: (0, 0)),
            _mod_spec(d, row_of_tile, mod_base),
            _mod_spec(d, row_of_tile, mod_base + 1),
            _mod_spec(d, row_of_tile, mod_base + 2),
            pl.BlockSpec((d, tf), lambda i, j: (0, j)),
            pl.BlockSpec((d, tf), lambda i, j: (0, j)),
            pl.BlockSpec((tf, d), lambda i, j: (j, 0)),
            pl.BlockSpec((1, d), lambda i, j: (0, 0)),
        ],
        out_specs=pl.BlockSpec((tm, d), lambda i, j: (i, 0)),
        out_shape=jax.ShapeDtypeStruct((t, d), F32),
        scratch_shapes=[pltpu.VMEM((tm, d), BF16)],
        compiler_params=_params(("parallel", "arbitrary"), 63),
        name="swiglu_half_step",
    )(x, g, mods, mods, mods, wg, wu, wd, fn)


def _rope_a(x, cos, sin):
    return x * cos + pltpu.roll(x, A_HEAD_DIM // 2, 1) * sin


def _rope_b(x, cos, sin_lo, sin_hi):
    half = B_ROPE // 2
    return x * cos + pltpu.roll(x, LANES - half, 1) * sin_lo + pltpu.roll(x, half, 1) * sin_hi


def _inproj_kernel(x_ref, g_ref, sh_ref, sc_ref, win_ref, qn_ref, wuq_ref, kvn_ref, wuk_ref, wuvt_ref,
                   cosa_ref, sina_ref, cosb_ref, sinlo_ref, sinhi_ref,
                   qa_ref, ka_ref, va_ref, qb_ref, kb_ref, vbt_ref):
    rb = min(INPROJ_ROWS, x_ref.shape[0])
    for r0 in range(0, x_ref.shape[0], rb):
        rows = slice(r0, r0 + rb)
        h = _rms_mod(x_ref[rows, :], g_ref[...], sh_ref[0], sc_ref[0]).astype(BF16)
        cosa, sina = cosa_ref[rows, :], sina_ref[rows, :]
        cosb, sinlo, sinhi = cosb_ref[rows, :], sinlo_ref[rows, :], sinhi_ref[rows, :]

        aq = _dot(h, win_ref[:, 0:OFF_AK]) * (A_HEAD_DIM ** -0.5 * LOG2_E)
        for s in range(A_HEADS):
            sl = slice(s * LANES, (s + 1) * LANES)
            qa_ref[rows, sl] = _rope_a(aq[:, sl], cosa, sina).astype(BF16)
        akv = _dot(h, win_ref[:, OFF_AK:OFF_CQ])
        for s in range(A_KV_HEADS):
            sl = slice(s * LANES, (s + 1) * LANES)
            ka_ref[rows, sl] = _rope_a(akv[:, sl], cosa, sina).astype(BF16)
        va_ref[rows, :] = akv[:, A_KV_W:].astype(BF16)

        cq = _dot(h, win_ref[:, OFF_CQ:OFF_CKV])
        cqn = (cq * lax.rsqrt(jnp.mean(cq * cq, axis=-1, keepdims=True) + EPS) * qn_ref[...]).astype(BF16)
        qb = _dot(cqn, wuq_ref[...]) * (B_QK ** -0.5 * LOG2_E)
        for hd in range(B_HEADS):
            lo = hd * B_PAD
            qb_ref[rows, lo:lo + LANES] = qb[:, lo:lo + LANES].astype(BF16)
            qb_ref[rows, lo + LANES:lo + B_PAD] = _rope_b(qb[:, lo + LANES:lo + B_PAD], cosb, sinlo,
                                                          sinhi).astype(BF16)

        ckr = _dot(h, win_ref[:, OFF_CKV:IN_W_PAD])
        ckv = ckr[:, 0:B_KV_RANK]
        ckvn = (ckv * lax.rsqrt(jnp.mean(ckv * ckv, axis=-1, keepdims=True) + EPS) * kvn_ref[...]).astype(BF16)
        kr = _rope_b(ckr[:, B_KV_RANK:], cosb, sinlo, sinhi).astype(BF16)
        kn = _dot(ckvn, wuk_ref[...])
        for hd in range(B_HEADS):
            lo = hd * B_PAD
            kb_ref[rows, lo:lo + LANES] = kn[:, hd * B_NOPE:(hd + 1) * B_NOPE].astype(BF16)
            kb_ref[rows, lo + LANES:lo + B_PAD] = kr
        vt = _dot_nt(wuvt_ref[...], ckvn).astype(BF16)
        ones_row = (lax.broadcasted_iota(jnp.int32, (B_VA - B_V, rb), 0) == 0).astype(F32).astype(BF16)
        for hd in range(B_HEADS):
            vbt_ref[0, hd * B_VA:hd * B_VA + B_V, rows] = vt[hd * B_V:(hd + 1) * B_V]
            vbt_ref[0, hd * B_VA + B_V:(hd + 1) * B_VA, rows] = ones_row


def _inproj(x, mods, row_of_tile, g, win, qn, wuq, kvn, wuk, wuvt, tables, table_tile, *, tm):
    t, d = x.shape
    tok = lambda w: pl.BlockSpec((tm, w), lambda i: (i, 0))
    tab = pl.BlockSpec((tm, LANES), lambda i: (table_tile(i), 0))
    widths = (A_Q_W, A_KV_W, A_KV_W, B_HEADS * B_PAD, B_HEADS * B_PAD)
    vt_rows = B_HEADS * B_VA
    return pl.pallas_call(
        _inproj_kernel,
        grid=(t // tm,),
        in_specs=[
            tok(d),
            _const_spec((1, d)),
            _mod_spec(d, row_of_tile, 3),
            _mod_spec(d, row_of_tile, 4),
            _const_spec(win.shape),
            _const_spec(qn.shape),
            _const_spec(wuq.shape),
            _const_spec(kvn.shape),
            _const_spec(wuk.shape),
            _const_spec(wuvt.shape),
            tab, tab, tab, tab, tab,
        ],
        out_specs=[tok(w) for w in widths] + [pl.BlockSpec((1, vt_rows, tm), lambda i: (i, 0, 0))],
        out_shape=[jax.ShapeDtypeStruct((t, w), BF16) for w in widths]
        + [jax.ShapeDtypeStruct((t // tm, vt_rows, tm), BF16)],
        compiler_params=_params(("parallel",), 56),
        name="attn_in_proj",
    )(x, g, mods, mods, win, qn, wuq, kvn, wuk, wuvt, *tables)


def _window_kernel(sink_ref, q_ref, k_ref, v_ref, kc_ref, vc_ref, o_ref, *, tq, n):
    qi = pl.program_id(1)
    span = tq + 2 * A_WINDOW
    start = pl.multiple_of(jnp.clip(qi * tq - A_WINDOW, 0, n - span), A_WINDOW)
    kpos = start + lax.broadcasted_iota(jnp.int32, (span, tq), 0)
    qpos = qi * tq + lax.broadcasted_iota(jnp.int32, (span, tq), 1)
    bias = jnp.where(jnp.abs(kpos - qpos) <= A_WINDOW, 0.0, NEG_BIG)
    for g in range(A_KV_HEADS):
        gl = slice(g * A_HEAD_DIM, (g + 1) * A_HEAD_DIM)
        heads = [slice((g * A_REP + r) * A_HEAD_DIM, (g * A_REP + r + 1) * A_HEAD_DIM) for r in range(A_REP)]
        k = k_ref[0, pl.ds(start, span), gl]
        v = v_ref[0, pl.ds(start, span), gl]
        q = jnp.concatenate([q_ref[0, :, hl] for hl in heads], axis=0)
        s_loc = _dot_nt(k, q)
        s_loc = jnp.concatenate([s_loc[:, r * tq:(r + 1) * tq] + bias for r in range(A_REP)], axis=1)
        s_ctx = _dot_nt(kc_ref[0, :, gl], q)
        sink = jnp.concatenate([jnp.full((1, tq), sink_ref[g * A_REP + r] * LOG2_E, F32) for r in range(A_REP)],
                               axis=1)
        m = jnp.maximum(jnp.maximum(jnp.max(s_loc, axis=0, keepdims=True),
                                    jnp.max(s_ctx, axis=0, keepdims=True)), sink)
        p_loc = jnp.exp2(s_loc - m)
        p_ctx = jnp.exp2(s_ctx - m)
        denom = (jnp.sum(p_loc, axis=0, keepdims=True) + jnp.sum(p_ctx, axis=0, keepdims=True)
                 + jnp.exp2(sink - m))
        o_t = _dot_tn(v, p_loc.astype(BF16)) + _dot_tn(vc_ref[0, :, gl], p_ctx.astype(BF16))
        o = jnp.transpose(o_t / denom)
        for r, hl in enumerate(heads):
            o_ref[0, :, hl] = o[r * tq:(r + 1) * tq].astype(BF16)


def _window_gqa(qa, ka, va, kca, vca, sink, *, tq):
    b, n, _ = qa.shape
    nc = kca.shape[1]
    assert n >= tq + 2 * A_WINDOW
    return pl.pallas_call(
        functools.partial(_window_kernel, tq=tq, n=n),
        grid=(b, n // tq),
        in_specs=[
            pl.BlockSpec(memory_space=pltpu.SMEM),
            pl.BlockSpec((1, tq, A_Q_W), lambda bi, qi: (bi, qi, 0)),
            pl.BlockSpec((1, n, A_KV_W), lambda bi, qi: (bi, 0, 0)),
            pl.BlockSpec((1, n, A_KV_W), lambda bi, qi: (bi, 0, 0)),
            pl.BlockSpec((1, nc, A_KV_W), lambda bi, qi: (bi, 0, 0)),
            pl.BlockSpec((1, nc, A_KV_W), lambda bi, qi: (bi, 0, 0)),
        ],
        out_specs=pl.BlockSpec((1, tq, A_Q_W), lambda bi, qi: (bi, qi, 0)),
        out_shape=jax.ShapeDtypeStruct((b, n, A_Q_W), BF16),
        compiler_params=_params(("parallel", "arbitrary"), 48),
        name="window_gqa",
    )(sink, qa, ka, va, kca, vca)


def _mla_kernel(q_ref, k_ref, vt_ref, kc_ref, vct_ref, *rest, tk, n_cast):
    cast_src, o_ref, cast_dst = rest[:n_cast], rest[n_cast], rest[n_cast + 1:2 * n_cast + 1]
    sa_ref, sb_ref, qt_ref = rest[2 * n_cast + 1:]
    for src, dst in zip(cast_src, cast_dst):
        dst[...] = src[...].astype(BF16)

    qt_ref[...] = q_ref[0].T
    qt = qt_ref[...]
    nchunks = vt_ref.shape[1]

    def stage_scores(buf, c):
        s = _dot(k_ref[0, pl.ds(pl.multiple_of(c * tk, tk), tk), :], qt)
        buf[...] = s
        return jnp.max(s, axis=0, keepdims=True)

    def probs(s, m):
        return jnp.exp2(s - m).astype(BF16)

    def update(s, s_max, vt, m_old, acc_old):
        m = jnp.maximum(m_old, s_max)
        acc = jnp.exp2(m_old - m) * acc_old + _dot(vt, probs(s, m))
        return m, acc

    bufs = (sa_ref, sb_ref)
    next_max = stage_scores(bufs[0], 0)
    s = _dot(kc_ref[0], qt)
    m = jnp.max(s, axis=0, keepdims=True)
    carry = (m, _dot(vct_ref[0, 0], probs(s, m)), next_max)

    def group(c0, carry, last):
        m, acc, cur_max = carry
        for j in range(MLA_GROUP):
            next_max = cur_max
            if not (last and j == MLA_GROUP - 1):
                next_max = stage_scores(bufs[(j + 1) % 2], c0 + j + 1)
            m, acc = update(bufs[j % 2][...], cur_max, vt_ref[0, c0 + j], m, acc)
            cur_max = next_max
        return m, acc, cur_max

    carry = lax.fori_loop(0, nchunks // MLA_GROUP - 1,
                          lambda i, cr: group(i * MLA_GROUP, cr, False), carry)
    _, acc, _ = group(nchunks - MLA_GROUP, carry, True)
    o_ref[0] = jnp.transpose(acc[0:B_V] / acc[B_V:B_V + 1]).astype(BF16)


def _cast_split(rows, cols, steps):
    for cb in range(1, steps + 1):
        rb = steps // cb
        if (steps % cb == 0 and rows % rb == 0 and cols % cb == 0
                and (rows // rb) % BF16_SUBLANES == 0 and (cols // cb) % LANES == 0):
            return rb, cb
    raise ValueError(f"cannot split a ({rows}, {cols}) matrix into {steps} aligned blocks")


def _mla_attention(qb, kb, vbt, kcb, vcbt, casts, *, tq):
    b, n, _ = qb.shape
    nc = kcb.shape[1]
    nchunks, _, tk = vbt.shape[1:]
    nq = n // tq
    assert nchunks * tk == n and nchunks % MLA_GROUP == 0 and vcbt.shape[1] == 1 and vcbt.shape[3] == nc
    steps = b * B_HEADS * nq

    cast_in, cast_out, cast_shapes = [], [], []
    for w, layer in casts:
        _, rows, cols = w.shape
        rb, cb = _cast_split(rows, cols, steps)
        blk = (rows // rb, cols // cb)
        step = lambda bi, hi, qi: (bi * B_HEADS + hi) * nq + qi
        cast_in.append(pl.BlockSpec((None,) + blk,
                                    lambda bi, hi, qi, layer=layer, cb=cb: (layer, step(bi, hi, qi) // cb,
                                                                            step(bi, hi, qi) % cb)))
        cast_out.append(pl.BlockSpec(blk, lambda bi, hi, qi, cb=cb: (step(bi, hi, qi) // cb, step(bi, hi, qi) % cb)))
        cast_shapes.append(jax.ShapeDtypeStruct((rows, cols), BF16))

    out = pl.pallas_call(
        functools.partial(_mla_kernel, tk=tk, n_cast=len(casts)),
        grid=(b, B_HEADS, nq),
        in_specs=[
            pl.BlockSpec((1, tq, B_PAD), lambda bi, hi, qi: (bi, qi, hi)),
            pl.BlockSpec((1, n, B_PAD), lambda bi, hi, qi: (bi, 0, hi)),
            pl.BlockSpec((1, nchunks, B_VA, tk), lambda bi, hi, qi: (bi, 0, hi, 0)),
            pl.BlockSpec((1, nc, B_PAD), lambda bi, hi, qi: (bi, 0, hi)),
            pl.BlockSpec((1, 1, B_VA, nc), lambda bi, hi, qi: (bi, 0, hi, 0)),
        ] + cast_in,
        out_specs=[pl.BlockSpec((1, tq, B_V), lambda bi, hi, qi: (bi, qi, hi))] + cast_out,
        out_shape=[jax.ShapeDtypeStruct((b, n, B_HEADS * B_V), BF16)] + cast_shapes,
        scratch_shapes=[pltpu.VMEM((tk, tq), F32), pltpu.VMEM((tk, tq), F32), pltpu.VMEM((B_PAD, tq), BF16)],
        compiler_params=_params(("arbitrary", "arbitrary", "arbitrary"), 48),
        name="mla_attention",
    )(qb, kb, vbt, kcb, vcbt, *[w for w, _ in casts])
    return out[0], out[1:]


def _outproj_kernel(x_ref, oa_ref, ob_ref, w_ref, gt_ref, o_ref):
    ka = oa_ref.shape[1]
    y = _dot(oa_ref[...], w_ref[0:ka, :]) + _dot(ob_ref[...], w_ref[ka:, :])
    o_ref[...] = x_ref[...] + gt_ref[0] * y


def _outproj(x, oa, ob, w, mods, row_of_tile, *, tm):
    t, d = x.shape
    return pl.pallas_call(
        _outproj_kernel,
        grid=(t // tm,),
        in_specs=[
            pl.BlockSpec((tm, d), lambda i: (i, 0)),
            pl.BlockSpec((tm, oa.shape[1]), lambda i: (i, 0)),
            pl.BlockSpec((tm, ob.shape[1]), lambda i: (i, 0)),
            _const_spec(w.shape),
            _mod_spec(d, row_of_tile, 5),
        ],
        out_specs=pl.BlockSpec((tm, d), lambda i: (i, 0)),
        out_shape=jax.ShapeDtypeStruct((t, d), F32),
        compiler_params=_params(("parallel",), 48),
        name="attn_out_proj",
    )(x, oa, ob, w, mods)


def _pool_kernel(x_ref, xp_ref, xn_ref, g_ref, sh_ref, sc_ref, gt_ref, w_ref, ps_ref, o_ref, hp_ref,
                 *, tm, n):
    i = pl.program_id(1)
    last = pl.num_programs(1) - 1
    g, sh, sc = g_ref[...], sh_ref[0], sc_ref[0]
    x = x_ref[0]
    hp_ref[0:POOL_HALO, :] = jnp.where(i > 0, _rms_mod(xp_ref[0], g, sh, sc), 0.0)
    hp_ref[POOL_HALO:POOL_HALO + tm, :] = _rms_mod(x, g, sh, sc)
    hp_ref[POOL_HALO + tm:, :] = jnp.where(i < last, _rms_mod(xn_ref[0], g, sh, sc), 0.0)

    t = i * tm + lax.broadcasted_iota(jnp.int32, (tm, 1), 0)
    gw = x.shape[1] // len(POOL_WINDOWS)
    for gi, w in enumerate(POOL_WINDOWS):
        cols = slice(gi * gw, (gi + 1) * gw)
        rows = tm + 2 * POOL_HALO
        shift_up = lambda a, k: a if k == 0 else pltpu.roll(a, rows - k, 0)
        tot = shift_up(hp_ref[:, cols], POOL_HALO - w // 2)
        k = 1
        while k < w:
            tot = tot + shift_up(tot, k)
            k *= 2
        tot = tot[0:tm]
        cnt =(jnp.minimum(t - w // 2 + w, n) - jnp.maximum(t - w // 2, 0)).astype(F32)
        diff = tot / cnt - hp_ref[pl.ds(POOL_HALO, tm), cols]
        y = _dot(diff.astype(BF16), w_ref[gi]) * ps_ref[:, cols]
        o_ref[0, :, cols] = x[:, cols] + gt_ref[0][:, cols] * y


def _pool_mixer(x3, mods, g, w, ps, *, tm):
    b, n, d = x3.shape
    hb = tm // POOL_HALO
    mod = lambda k: pl.BlockSpec((1, 1, d), lambda bi, i: (bi * N_MOD + k, 0, 0))
    return pl.pallas_call(
        functools.partial(_pool_kernel, tm=tm, n=n),
        grid=(b, n // tm),
        in_specs=[
            pl.BlockSpec((1, tm, d), lambda bi, i: (bi, i, 0)),
            pl.BlockSpec((1, POOL_HALO, d), lambda bi, i: (bi, jnp.maximum(i * hb - 1, 0), 0)),
            pl.BlockSpec((1, POOL_HALO, d), lambda bi, i: (bi, jnp.minimum((i + 1) * hb, n // POOL_HALO - 1), 0)),
            pl.BlockSpec((1, d), lambda bi, i: (0, 0)),
            mod(3), mod(4), mod(5),
            pl.BlockSpec(w.shape, lambda bi, i: (0, 0, 0)),
            pl.BlockSpec((1, d), lambda bi, i: (0, 0)),
        ],
        out_specs=pl.BlockSpec((1, tm, d), lambda bi, i: (bi, i, 0)),
        out_shape=jax.ShapeDtypeStruct((b, n, d), F32),
        scratch_shapes=[pltpu.VMEM((tm + 2 * POOL_HALO, d), F32)],
        compiler_params=_params(("parallel", "arbitrary"), 48),
        name="pool_mixer",
    )(x3, x3, x3, g, mods, mods, mods, w, ps)


def _rope_tables(n):
    rows = n // GRID_W
    row = jnp.arange(rows, dtype=F32)
    col = jnp.arange(GRID_W, dtype=F32)

    def cos_sin(rot_dim):
        nf = rot_dim // 4
        inv = ROPE_BASE ** (-jnp.arange(nf, dtype=F32) / nf)
        expand = lambda fn: jnp.concatenate([jnp.repeat(fn(row[:, None] * inv), GRID_W, axis=0),
                                             jnp.tile(fn(col[:, None] * inv), (rows, 1))], axis=-1)
        return expand(jnp.cos), expand(jnp.sin)

    ca, sa = cos_sin(A_HEAD_DIM)
    cb, sb = cos_sin(B_ROPE)
    zb = jnp.zeros_like(cb)
    zpad = jnp.zeros((n, LANES - B_ROPE), F32)
    return (jnp.concatenate([ca, ca], axis=-1), jnp.concatenate([-sa, sa], axis=-1),
            jnp.concatenate([cb, cb, zpad], axis=-1), jnp.concatenate([-sb, zb, zpad], axis=-1),
            jnp.concatenate([zb, sb, zpad], axis=-1))


def _identity_tables(n):
    one = jnp.ones((n, LANES), F32)
    zero = jnp.zeros((n, LANES), F32)
    return (one, zero, one, zero, zero)


def _attn_weights(w_in, w_uq, w_ukv):
    d = w_in.shape[0]
    win = jnp.concatenate([w_in, jnp.zeros((d, IN_W_PAD - IN_W), w_in.dtype)], axis=1).astype(BF16)
    uq = w_uq.reshape(B_Q_RANK, B_HEADS, B_QK)
    uq = jnp.concatenate([uq, jnp.zeros((B_Q_RANK, B_HEADS, B_PAD - B_QK), uq.dtype)], axis=-1)
    ukv = w_ukv.reshape(B_KV_RANK, B_HEADS, B_NOPE + B_V)
    uk = ukv[:, :, :B_NOPE].reshape(B_KV_RANK, B_HEADS * B_NOPE)
    uvt = ukv[:, :, B_NOPE:].reshape(B_KV_RANK, B_HEADS * B_V).T
    return win, uq.reshape(B_Q_RANK, B_HEADS * B_PAD).astype(BF16), uk.astype(BF16), uvt.astype(BF16)


def kernel(x, c, ctx, c_ctx, w_ada, b_ada, norm_ffn1, norm_mix, norm_ffn2, ffn1_w_gate, ffn1_w_up, ffn1_w_down,
           ffn2_w_gate, ffn2_w_up, ffn2_w_down, attn_w_in, attn_sink, mla_q_norm, mla_w_uq, mla_kv_norm, mla_w_ukv,
           attn_w_out, pool_w, pool_scale, final_norm):
    b, n, d = x.shape
    nc = ctx.shape[1]
    depth = w_ada.shape[0]
    f = ffn1_w_gate.shape[2]
    t = b * n

    tm = min(512, n)
    tm_ffn = min(1024, n)
    tf = min(512, f)
    tmc = min(512, b * nc)
    tq_a = min(256, n)
    tq_b = min(1024, n)
    assert n % tm == 0 and f % tf == 0 and (b * nc) % tmc == 0 and n % GRID_W == 0

    tiles_per_sample = n // tm
    x_row = lambda i: i // tiles_per_sample
    ffn_row = lambda i: i // (n // tm_ffn)
    ctx_row = lambda i: b

    mods = _mods(c, c_ctx, w_ada, b_ada)
    rope = _rope_tables(n)
    fn = final_norm.reshape(1, d)

    ffn_f32 = ((ffn1_w_gate, ffn1_w_up, ffn1_w_down), (ffn2_w_gate, ffn2_w_up, ffn2_w_down))
    ffn_bf16 = {}

    def ffn_weights(family, layer):
        if (family, layer) not in ffn_bf16:
            ffn_bf16[(family, layer)] = tuple(w[layer].astype(BF16) for w in ffn_f32[family])
        return ffn_bf16[(family, layer)]

    xs = x.reshape(t, d)
    cx = ctx.reshape(b * nc, d)
    for l in range(depth):
        is_attn = l % 2 == 0
        ctx_out = any(j % 2 == 0 for j in range(l + 1, depth))
        ctx_in = is_attn or ctx_out
        i = l // 2
        m = mods[l]
        g1, gm, g2 = (v[l].reshape(1, d) for v in (norm_ffn1, norm_mix, norm_ffn2))
        w1 = ffn_weights(0, l)

        xs = _ffn(xs, m, 0, ffn_row, g1, *w1, fn, tm=tm_ffn, tf=tf, final_norm=False)
        if ctx_in:
            cx = _ffn(cx, m, 0, ctx_row, g1, *w1, fn, tm=tmc, tf=tf, final_norm=False)

        if is_attn:
            win, wuq, wuk, wuvt = _attn_weights(attn_w_in[i], mla_w_uq[i], mla_w_ukv[i])
            qn = mla_q_norm[i].reshape(1, -1)
            kvn = mla_kv_norm[i].reshape(1, -1)
            qa, ka, va, qb, kb, vbt = _inproj(xs, m, x_row, gm, win, qn, wuq, kvn, wuk, wuvt, rope,
                                              lambda ti: ti % tiles_per_sample, tm=tm)
            _, kca, vca, _, kcb, vcbt = _inproj(cx, m, ctx_row, gm, win, qn, wuq, kvn, wuk, wuvt,
                                                _identity_tables(nc), lambda ti: 0, tm=nc)
            r3 = lambda a, rows: a.reshape(b, rows, a.shape[-1])
            r4 = lambda a: a.reshape(b, a.shape[0] // b, a.shape[1], a.shape[2])
            oa = _window_gqa(r3(qa, n), r3(ka, n), r3(va, n), r3(kca, nc), r3(vca, nc), attn_sink[i], tq=tq_a)
            pending = [(fam, lj) for lj in range(l, depth) for fam in (0, 1) if (lj, fam) > (l, 0)]
            ob, cast = _mla_attention(r3(qb, n), r3(kb, n), r4(vbt), r3(kcb, nc), r4(vcbt),
                                      [(w, lj) for fam, lj in pending for w in ffn_f32[fam]], tq=tq_b)
            for k, key in enumerate(pending):
                ffn_bf16[key] = tuple(cast[3 * k:3 * k + 3])
            xs = _outproj(xs, oa.reshape(t, -1), ob.reshape(t, -1), attn_w_out[i].astype(BF16), m, x_row, tm=tm)
            if ctx_out:
                raise NotImplementedError("context-stream attention output is not needed at this depth")
        else:
            xs = _pool_mixer(xs.reshape(b, n, d), m, gm, pool_w[i].astype(BF16), pool_scale[i].reshape(1, d),
                             tm=tm).reshape(t, d)
            if ctx_out:
                raise NotImplementedError("context-stream pooling output is not needed at this depth")

        w2 = ffn_weights(1, l)
        xs = _ffn(xs, m, 6, ffn_row, g2, *w2, fn, tm=tm_ffn, tf=tf, final_norm=(l == depth - 1))
        if ctx_out:
            cx = _ffn(cx, m, 6, ctx_row, g2, *w2, fn, tm=tmc, tf=tf, final_norm=False)
    return xs.reshape(b, n, d)
```

```python
import functools

import jax
import jax.numpy as jnp
from jax import lax
from jax.experimental import pallas as pl
from jax.experimental.pallas import tpu as pltpu

F32 = jnp.float32
BF16 = jnp.bfloat16

LANES = 128
SUBLANES = 8
BF16_SUBLANES = 16
ROW_CHUNK = BF16_SUBLANES
NORM_COL_BLOCK = 4 * LANES
MIB = 1 << 20

EPS = 1e-6
ROPE_BASE = 10000.0
GRID_W = 64
N_MOD = 9
A_HEADS = 8
A_KV_HEADS = 2
A_REP = A_HEADS // A_KV_HEADS
A_HEAD_DIM = 128
A_WINDOW = 128
B_HEADS = 8
B_NOPE = 128
B_ROPE = 64
B_QK = B_NOPE + B_ROPE
B_V = 128
B_Q_RANK = 768
B_KV_RANK = 256
B_PAD = 2 * LANES
B_VA = B_V + BF16_SUBLANES
A_Q_W = A_HEADS * A_HEAD_DIM
A_KV_W = A_KV_HEADS * A_HEAD_DIM
OFF_AK = A_Q_W
OFF_AV = OFF_AK + A_KV_W
OFF_CQ = OFF_AV + A_KV_W
OFF_CKV = OFF_CQ + B_Q_RANK
OFF_KR = OFF_CKV + B_KV_RANK
IN_W = OFF_KR + B_ROPE
IN_W_PAD = OFF_KR + LANES
POOL_WINDOWS = (2, 4, 8, 16)
POOL_HALO = SUBLANES
NEG_BIG = -1e30
LOG2_E = 1.4426950408889634
FFN_FIRST_STEP_ROWS = 256
INPROJ_ROWS = 512
MLA_GROUP = 4

MOD_ROWS = 8


def _params(semantics, vmem_mib):
    return pltpu.CompilerParams(dimension_semantics=semantics, vmem_limit_bytes=vmem_mib * MIB)


def _const_spec(shape):
    nd = len(shape)
    return pl.BlockSpec(shape, lambda *_: (0,) * nd, pipeline_mode=pl.Buffered(1))


def _rms_mod(x, g, shift, scale):
    y = x * lax.rsqrt(jnp.mean(x * x, axis=-1, keepdims=True) + EPS)
    return (y * g) * (1.0 + scale) + shift


def _rms_mod_rows(x_ref, g_ref, sh_ref, sc_ref, r_ref, h_ref, copy_ref=None):
    rows, d = x_ref.shape
    n_chunks = rows // ROW_CHUNK
    unroll = min(8, n_chunks)
    rows_of = lambda i: pl.ds(pl.multiple_of(i * ROW_CHUNK, ROW_CHUNK), ROW_CHUNK)

    def partial_sums(i, carry):
        x = x_ref[rows_of(i), :]
        acc = x[:, 0:LANES] * x[:, 0:LANES]
        for k in range(1, d // LANES):
            xk = x[:, k * LANES:(k + 1) * LANES]
            acc = acc + xk * xk
        r_ref[rows_of(i), :] = acc
        return carry

    lax.fori_loop(0, n_chunks, partial_sums, 0, unroll=unroll)
    mean_sq = jnp.sum(r_ref[...], axis=-1, keepdims=True) * (1.0 / d)
    r_ref[...] = jnp.broadcast_to(lax.rsqrt(mean_sq + EPS), r_ref.shape)

    gain = g_ref[...] * (1.0 + sc_ref[0])
    shift = sh_ref[0]
    cb = min(d, NORM_COL_BLOCK)
    for c0 in range(0, d, cb):
        cols = slice(c0, c0 + cb)
        gain_c, shift_c = gain[:, cols], shift[:, cols]

        def apply(i, carry, cols=cols, gain_c=gain_c, shift_c=shift_c):
            x = x_ref[rows_of(i), cols]
            r = jnp.concatenate([r_ref[rows_of(i), :]] * (cb // LANES), axis=1)
            h_ref[rows_of(i), cols] = ((x * r) * gain_c + shift_c).astype(BF16)
            if copy_ref is not None:
                copy_ref[rows_of(i), cols] = x
            return carry

        lax.fori_loop(0, n_chunks, apply, 0, unroll=unroll)


def _dot(a, b):
    return jnp.dot(a, b, preferred_element_type=F32)


def _dot_nt(a, b):
    return lax.dot_general(a, b, (((1,), (1,)), ((), ())), preferred_element_type=F32)


def _dot_tn(a, b):
    return lax.dot_general(a, b, (((0,), (0,)), ((), ())), preferred_element_type=F32)


def _mods_kernel(c_ref, w_ref, b_ref, o_ref):
    c = c_ref[...]
    a = (c * jax.nn.sigmoid(c)).astype(BF16)
    o_ref[0] = _dot(a, w_ref[0].astype(BF16)) + b_ref[0]


def _mods(c, c_ctx, w_ada, b_ada):
    depth, d, nd = w_ada.shape
    b = c.shape[0]
    assert b + 1 <= MOD_ROWS
    cp = jnp.concatenate([c, c_ctx[None, :], jnp.zeros((MOD_ROWS - b - 1, d), F32)], axis=0)
    tn = min(d, 1024)
    assert nd % tn == 0
    out = pl.pallas_call(
        _mods_kernel,
        grid=(depth, nd // tn),
        in_specs=[
            pl.BlockSpec((MOD_ROWS, d), lambda l, j: (0, 0)),
            pl.BlockSpec((1, d, tn), lambda l, j: (l, 0, j)),
            pl.BlockSpec((1, 1, tn), lambda l, j: (l, 0, j)),
        ],
        out_specs=pl.BlockSpec((1, MOD_ROWS, tn), lambda l, j: (l, 0, j)),
        out_shape=jax.ShapeDtypeStruct((depth, MOD_ROWS, nd), F32),
        compiler_params=_params(("arbitrary", "arbitrary"), 40),
        name="adaln_mods",
    )(cp, w_ada, b_ada.reshape(depth, 1, nd))
    return out.reshape(depth, MOD_ROWS * N_MOD, 1, d)


def _mod_spec(d, row_of_tile, k):
    return pl.BlockSpec((1, 1, d), lambda i, *_: (row_of_tile(i) * N_MOD + k, 0, 0))


def _ffn_kernel(x_ref, g_ref, sh_ref, sc_ref, gt_ref, wg_ref, wu_ref, wd_ref, fn_ref,
                o_ref, h_ref, *, final_norm):
    j = pl.program_id(1)

    def accumulate(h, base):
        gate = _dot(h, wg_ref[...])
        up = _dot(h, wu_ref[...])
        a = (gate * jax.nn.sigmoid(gate)) * up
        return base + (0.5 * gt_ref[0]) * _dot(a.astype(BF16), wd_ref[...])

    @pl.when(j == 0)
    def _():
        gain = g_ref[...] * (1.0 + sc_ref[0])
        shift = sh_ref[0]
        rb = min(FFN_FIRST_STEP_ROWS, x_ref.shape[0])
        for r0 in range(0, x_ref.shape[0], rb):
            rows = slice(r0, r0 + rb)
            x = x_ref[rows, :]
            h = ((x * lax.rsqrt(jnp.mean(x * x, axis=-1, keepdims=True) + EPS)) * gain + shift).astype(BF16)
            h_ref[rows, :] = h
            o_ref[rows, :] = accumulate(h, x_ref[rows, :])

    @pl.when(j > 0)
    def _():
        o_ref[...] = accumulate(h_ref[...], o_ref[...])

    if final_norm:
        @pl.when(j == pl.num_programs(1) - 1)
        def _():
            out = o_ref[...]
            o_ref[...] = out * lax.rsqrt(jnp.mean(out * out, axis=-1, keepdims=True) + EPS) * fn_ref[...]


def _ffn(x, mods, mod_base, row_of_tile, g, wg, wu, wd, fn, *, tm, tf, final_norm):
    t, d = x.shape
    f = wg.shape[1]
    chunk = lambda i, j: jnp.where(i % 2 == 0, j, f // tf - 1 - j)
    return pl.pallas_call(
        functools.partial(_ffn_kernel, final_norm=final_norm),
        grid=(t // tm, f // tf),
        in_specs=[
            pl.BlockSpec((tm, d), lambda i, j: (i, 0)),
            pl.BlockSpec((1, d), lambda i, j: (0, 0)),
            _mod_spec(d, row_of_tile, mod_base),
            _mod_spec(d, row_of_tile, mod_base + 1),
            _mod_spec(d, row_of_tile, mod_base + 2),
            pl.BlockSpec((d, tf), lambda i, j: (0, chunk(i, j))),
            pl.BlockSpec((d, tf), lambda i, j: (0, chunk(i, j))),
            pl.BlockSpec((tf, d), lambda i, j: (chunk(i, j), 0)),
            pl.BlockSpec((1, d), lambda i, j: (0, 0)),
        ],
        out_specs=pl.BlockSpec((tm, d), lambda i, j: (i, 0)),
        out_shape=jax.ShapeDtypeStruct((t, d), F32),
        scratch_shapes=[pltpu.VMEM((tm, d), BF16)],
        compiler_params=_params(("parallel", "arbitrary"), 63),
        name="swiglu_half_step",
    )(x, g, mods, mods, mods, wg, wu, wd, fn)


def _rope_a(x, cos, sin):
    return x * cos + pltpu.roll(x, A_HEAD_DIM // 2, 1) * sin


def _rope_b(x, cos, sin_lo, sin_hi):
    half = B_ROPE // 2
    return x * cos + pltpu.roll(x, LANES - half, 1) * sin_lo + pltpu.roll(x, half, 1) * sin_hi


def _inproj_kernel(x_ref, g_ref, sh_ref, sc_ref, win_ref, qn_ref, wuq_ref, kvn_ref, wuk_ref, wuvt_ref,
                   cosa_ref, sina_ref, cosb_ref, sinlo_ref, sinhi_ref,
                   qa_ref, ka_ref, va_ref, qb_ref, kb_ref, vbt_ref):
    rb = min(INPROJ_ROWS, x_ref.shape[0])
    for r0 in range(0, x_ref.shape[0], rb):
        rows = slice(r0, r0 + rb)
        h = _rms_mod(x_ref[rows, :], g_ref[...], sh_ref[0], sc_ref[0]).astype(BF16)
        cosa, sina = cosa_ref[rows, :], sina_ref[rows, :]
        cosb, sinlo, sinhi = cosb_ref[rows, :], sinlo_ref[rows, :], sinhi_ref[rows, :]

        aq = _dot(h, win_ref[:, 0:OFF_AK]) * (A_HEAD_DIM ** -0.5 * LOG2_E)
        for s in range(A_HEADS):
            sl = slice(s * LANES, (s + 1) * LANES)
            qa_ref[rows, sl] = _rope_a(aq[:, sl], cosa, sina).astype(BF16)
        akv = _dot(h, win_ref[:, OFF_AK:OFF_CQ])
        for s in range(A_KV_HEADS):
            sl = slice(s * LANES, (s + 1) * LANES)
            ka_ref[rows, sl] = _rope_a(akv[:, sl], cosa, sina).astype(BF16)
        va_ref[rows, :] = akv[:, A_KV_W:].astype(BF16)

        cq = _dot(h, win_ref[:, OFF_CQ:OFF_CKV])
        cqn = (cq * lax.rsqrt(jnp.mean(cq * cq, axis=-1, keepdims=True) + EPS) * qn_ref[...]).astype(BF16)
        qb = _dot(cqn, wuq_ref[...]) * (B_QK ** -0.5 * LOG2_E)
        for hd in range(B_HEADS):
            lo = hd * B_PAD
            qb_ref[rows, lo:lo + LANES] = qb[:, lo:lo + LANES].astype(BF16)
            qb_ref[rows, lo + LANES:lo + B_PAD] = _rope_b(qb[:, lo + LANES:lo + B_PAD], cosb, sinlo,
                                                          sinhi).astype(BF16)

        ckr = _dot(h, win_ref[:, OFF_CKV:IN_W_PAD])
        ckv = ckr[:, 0:B_KV_RANK]
        ckvn = (ckv * lax.rsqrt(jnp.mean(ckv * ckv, axis=-1, keepdims=True) + EPS) * kvn_ref[...]).astype(BF16)
        kr = _rope_b(ckr[:, B_KV_RANK:], cosb, sinlo, sinhi).astype(BF16)
        kn = _dot(ckvn, wuk_ref[...])
        for hd in range(B_HEADS):
            lo = hd * B_PAD
            kb_ref[rows, lo:lo + LANES] = kn[:, hd * B_NOPE:(hd + 1) * B_NOPE].astype(BF16)
            kb_ref[rows, lo + LANES:lo + B_PAD] = kr
        vt = _dot_nt(wuvt_ref[...], ckvn).astype(BF16)
        ones_row = (lax.broadcasted_iota(jnp.int32, (B_VA - B_V, rb), 0) == 0).astype(F32).astype(BF16)
        for hd in range(B_HEADS):
            vbt_ref[0, hd * B_VA:hd * B_VA + B_V, rows] = vt[hd * B_V:(hd + 1) * B_V]
            vbt_ref[0, hd * B_VA + B_V:(hd + 1) * B_VA, rows] = ones_row


def _inproj(x, mods, row_of_tile, g, win, qn, wuq, kvn, wuk, wuvt, tables, table_tile, *, tm):
    t, d = x.shape
    tok = lambda w: pl.BlockSpec((tm, w), lambda i: (i, 0))
    tab = pl.BlockSpec((tm, LANES), lambda i: (table_tile(i), 0))
    widths = (A_Q_W, A_KV_W, A_KV_W, B_HEADS * B_PAD, B_HEADS * B_PAD)
    vt_rows = B_HEADS * B_VA
    return pl.pallas_call(
        _inproj_kernel,
        grid=(t // tm,),
        in_specs=[
            tok(d),
            _const_spec((1, d)),
            _mod_spec(d, row_of_tile, 3),
            _mod_spec(d, row_of_tile, 4),
            _const_spec(win.shape),
            _const_spec(qn.shape),
            _const_spec(wuq.shape),
            _const_spec(kvn.shape),
            _const_spec(wuk.shape),
            _const_spec(wuvt.shape),
            tab, tab, tab, tab, tab,
        ],
        out_specs=[tok(w) for w in widths] + [pl.BlockSpec((1, vt_rows, tm), lambda i: (i, 0, 0))],
        out_shape=[jax.ShapeDtypeStruct((t, w), BF16) for w in widths]
        + [jax.ShapeDtypeStruct((t // tm, vt_rows, tm), BF16)],
        compiler_params=_params(("parallel",), 56),
        name="attn_in_proj",
    )(x, g, mods, mods, win, qn, wuq, kvn, wuk, wuvt, *tables)


def _window_kernel(sink_ref, q_ref, k_ref, v_ref, kc_ref, vc_ref, o_ref, qt_ref, *, tq, n):
    qi = pl.program_id(1)
    span = tq + 2 * A_WINDOW
    start = pl.multiple_of(jnp.clip(qi * tq - A_WINDOW, 0, n - span), A_WINDOW)
    kpos = start + lax.broadcasted_iota(jnp.int32, (span, tq), 0)
    qpos = qi * tq + lax.broadcasted_iota(jnp.int32, (span, tq), 1)
    nc = kc_ref.shape[1]
    bias = jnp.concatenate([jnp.where(jnp.abs(kpos - qpos) <= A_WINDOW, 0.0, NEG_BIG), jnp.zeros((nc, tq), F32)],
                           axis=0)
    for g in range(A_KV_HEADS):
        gl = slice(g * A_HEAD_DIM, (g + 1) * A_HEAD_DIM)
        heads = [slice((g * A_REP + r) * A_HEAD_DIM, (g * A_REP + r + 1) * A_HEAD_DIM) for r in range(A_REP)]
        k = jnp.concatenate([k_ref[0, pl.ds(start, span), gl], kc_ref[0, :, gl]], axis=0)
        v = jnp.concatenate([v_ref[0, pl.ds(start, span), gl], vc_ref[0, :, gl]], axis=0)
        for r, hl in enumerate(heads):
            qt_ref[:, r * tq:(r + 1) * tq] = q_ref[0, :, hl].T
        s = _dot(k, qt_ref[...])
        s = jnp.concatenate([s[:, r * tq:(r + 1) * tq] + bias for r in range(A_REP)], axis=1)
        sink = jnp.concatenate([jnp.full((1, tq), sink_ref[g * A_REP + r] * LOG2_E, F32) for r in range(A_REP)],
                               axis=1)
        m = jnp.maximum(jnp.max(s, axis=0, keepdims=True), sink)
        p = jnp.exp2(s - m)
        denom = jnp.sum(p, axis=0, keepdims=True) + jnp.exp2(sink - m)
        o = jnp.transpose(_dot_tn(v, p.astype(BF16)) / denom)
        for r, hl in enumerate(heads):
            o_ref[0, :, hl] = o[r * tq:(r + 1) * tq].astype(BF16)


def _window_gqa(qa, ka, va, kca, vca, sink, *, tq):
    b, n, _ = qa.shape
    nc = kca.shape[1]
    assert n >= tq + 2 * A_WINDOW
    return pl.pallas_call(
        functools.partial(_window_kernel, tq=tq, n=n),
        grid=(b, n // tq),
        in_specs=[
            pl.BlockSpec(memory_space=pltpu.SMEM),
            pl.BlockSpec((1, tq, A_Q_W), lambda bi, qi: (bi, qi, 0)),
            pl.BlockSpec((1, n, A_KV_W), lambda bi, qi: (bi, 0, 0)),
            pl.BlockSpec((1, n, A_KV_W), lambda bi, qi: (bi, 0, 0)),
            pl.BlockSpec((1, nc, A_KV_W), lambda bi, qi: (bi, 0, 0)),
            pl.BlockSpec((1, nc, A_KV_W), lambda bi, qi: (bi, 0, 0)),
        ],
        out_specs=pl.BlockSpec((1, tq, A_Q_W), lambda bi, qi: (bi, qi, 0)),
        out_shape=jax.ShapeDtypeStruct((b, n, A_Q_W), BF16),
        scratch_shapes=[pltpu.VMEM((A_HEAD_DIM, A_REP * tq), BF16)],
        compiler_params=_params(("parallel", "arbitrary"), 48),
        name="window_gqa",
    )(sink, qa, ka, va, kca, vca)


def _mla_kernel(q_ref, k_ref, vt_ref, kc_ref, vct_ref, *rest, tk, n_cast):
    cast_src, o_ref, cast_dst = rest[:n_cast], rest[n_cast], rest[n_cast + 1:2 * n_cast + 1]
    sa_ref, sb_ref, qt_ref = rest[2 * n_cast + 1:]
    for src, dst in zip(cast_src, cast_dst):
        dst[...] = src[...].astype(BF16)

    qt_ref[...] = q_ref[0].T
    qt = qt_ref[...]
    nchunks = vt_ref.shape[1]

    def stage_scores(buf, c):
        s = _dot(k_ref[0, pl.ds(pl.multiple_of(c * tk, tk), tk), :], qt)
        buf[...] = s
        return jnp.max(s, axis=0, keepdims=True)

    def probs(s, m):
        return jnp.exp2(s - m).astype(BF16)

    def update(s, s_max, vt, m_old, acc_old):
        m = jnp.maximum(m_old, s_max)
        acc = jnp.exp2(m_old - m) * acc_old + _dot(vt, probs(s, m))
        return m, acc

    bufs = (sa_ref, sb_ref)
    next_max = stage_scores(bufs[0], 0)
    s = _dot(kc_ref[0], qt)
    m = jnp.max(s, axis=0, keepdims=True)
    carry = (m, _dot(vct_ref[0, 0], probs(s, m)), next_max)

    def group(c0, carry, last):
        m, acc, cur_max = carry
        for j in range(MLA_GROUP):
            next_max = cur_max
            if not (last and j == MLA_GROUP - 1):
                next_max = stage_scores(bufs[(j + 1) % 2], c0 + j + 1)
            m, acc = update(bufs[j % 2][...], cur_max, vt_ref[0, c0 + j], m, acc)
            cur_max = next_max
        return m, acc, cur_max

    carry = lax.fori_loop(0, nchunks // MLA_GROUP - 1,
                          lambda i, cr: group(i * MLA_GROUP, cr, False), carry)
    _, acc, _ = group(nchunks - MLA_GROUP, carry, True)
    o_ref[0] = jnp.transpose(acc[0:B_V] / acc[B_V:B_V + 1]).astype(BF16)


def _cast_split(rows, cols, steps):
    for cb in range(1, steps + 1):
        rb = steps // cb
        if (steps % cb == 0 and rows % rb == 0 and cols % cb == 0
                and (rows // rb) % BF16_SUBLANES == 0 and (cols // cb) % LANES == 0):
            return rb, cb
    raise ValueError(f"cannot split a ({rows}, {cols}) matrix into {steps} aligned blocks")


def _mla_attention(qb, kb, vbt, kcb, vcbt, casts, *, tq):
    b, n, _ = qb.shape
    nc = kcb.shape[1]
    nchunks, _, tk = vbt.shape[1:]
    nq = n // tq
    assert nchunks * tk == n and nchunks % MLA_GROUP == 0 and vcbt.shape[1] == 1 and vcbt.shape[3] == nc
    steps = b * B_HEADS * nq

    cast_in, cast_out, cast_shapes = [], [], []
    for w, layer in casts:
        _, rows, cols = w.shape
        rb, cb = _cast_split(rows, cols, steps)
        blk = (rows // rb, cols // cb)
        step = lambda bi, hi, qi: (bi * B_HEADS + hi) * nq + qi
        cast_in.append(pl.BlockSpec((None,) + blk,
                                    lambda bi, hi, qi, layer=layer, cb=cb: (layer, step(bi, hi, qi) // cb,
                                                                            step(bi, hi, qi) % cb)))
        cast_out.append(pl.BlockSpec(blk, lambda bi, hi, qi, cb=cb: (step(bi, hi, qi) // cb, step(bi, hi, qi) % cb)))
        cast_shapes.append(jax.ShapeDtypeStruct((rows, cols), BF16))

    out = pl.pallas_call(
        functools.partial(_mla_kernel, tk=tk, n_cast=len(casts)),
        grid=(b, B_HEADS, nq),
        in_specs=[
            pl.BlockSpec((1, tq, B_PAD), lambda bi, hi, qi: (bi, qi, hi)),
            pl.BlockSpec((1, n, B_PAD), lambda bi, hi, qi: (bi, 0, hi)),
            pl.BlockSpec((1, nchunks, B_VA, tk), lambda bi, hi, qi: (bi, 0, hi, 0)),
            pl.BlockSpec((1, nc, B_PAD), lambda bi, hi, qi: (bi, 0, hi)),
            pl.BlockSpec((1, 1, B_VA, nc), lambda bi, hi, qi: (bi, 0, hi, 0)),
        ] + cast_in,
        out_specs=[pl.BlockSpec((1, tq, B_V), lambda bi, hi, qi: (bi, qi, hi))] + cast_out,
        out_shape=[jax.ShapeDtypeStruct((b, n, B_HEADS * B_V), BF16)] + cast_shapes,
        scratch_shapes=[pltpu.VMEM((tk, tq), F32), pltpu.VMEM((tk, tq), F32), pltpu.VMEM((B_PAD, tq), BF16)],
        compiler_params=_params(("arbitrary", "arbitrary", "arbitrary"), 48),
        name="mla_attention",
    )(qb, kb, vbt, kcb, vcbt, *[w for w, _ in casts])
    return out[0], out[1:]


def _outproj_kernel(x_ref, oa_ref, ob_ref, w_ref, gt_ref, o_ref):
    ka = oa_ref.shape[1]
    y = _dot(oa_ref[...], w_ref[0:ka, :]) + _dot(ob_ref[...], w_ref[ka:, :])
    o_ref[...] = x_ref[...] + gt_ref[0] * y


def _outproj(x, oa, ob, w, mods, row_of_tile, *, tm):
    t, d = x.shape
    return pl.pallas_call(
        _outproj_kernel,
        grid=(t // tm,),
        in_specs=[
            pl.BlockSpec((tm, d), lambda i: (i, 0)),
            pl.BlockSpec((tm, oa.shape[1]), lambda i: (i, 0)),
            pl.BlockSpec((tm, ob.shape[1]), lambda i: (i, 0)),
            _const_spec(w.shape),
            _mod_spec(d, row_of_tile, 5),
        ],
        out_specs=pl.BlockSpec((tm, d), lambda i: (i, 0)),
        out_shape=jax.ShapeDtypeStruct((t, d), F32),
        compiler_params=_params(("parallel",), 48),
        name="attn_out_proj",
    )(x, oa, ob, w, mods)


def _pool_kernel(x_ref, xp_ref, xn_ref, g_ref, sh_ref, sc_ref, gt_ref, w_ref, ps_ref, o_ref, hp_ref,
                 *, tm, n):
    i = pl.program_id(1)
    last = pl.num_programs(1) - 1
    g, sh, sc = g_ref[...], sh_ref[0], sc_ref[0]
    x = x_ref[0]
    hp_ref[0:POOL_HALO, :] = jnp.where(i > 0, _rms_mod(xp_ref[0], g, sh, sc), 0.0)
    hp_ref[POOL_HALO:POOL_HALO + tm, :] = _rms_mod(x, g, sh, sc)
    hp_ref[POOL_HALO + tm:, :] = jnp.where(i < last, _rms_mod(xn_ref[0], g, sh, sc), 0.0)

    t = i * tm + lax.broadcasted_iota(jnp.int32, (tm, 1), 0)
    gw = x.shape[1] // len(POOL_WINDOWS)
    for gi, w in enumerate(POOL_WINDOWS):
        cols = slice(gi * gw, (gi + 1) * gw)
        rows = tm + 2 * POOL_HALO
        shift_up = lambda a, k: a if k == 0 else pltpu.roll(a, rows - k, 0)
        tot = shift_up(hp_ref[:, cols], POOL_HALO - w // 2)
        k = 1
        while k < w:
            tot = tot + shift_up(tot, k)
            k *= 2
        tot = tot[0:tm]
        cnt =(jnp.minimum(t - w // 2 + w, n) - jnp.maximum(t - w // 2, 0)).astype(F32)
        diff = tot / cnt - hp_ref[pl.ds(POOL_HALO, tm), cols]
        y = _dot(diff.astype(BF16), w_ref[gi]) * ps_ref[:, cols]
        o_ref[0, :, cols] = x[:, cols] + gt_ref[0][:, cols] * y


def _pool_mixer(x3, mods, g, w, ps, *, tm):
    b, n, d = x3.shape
    hb = tm // POOL_HALO
    mod = lambda k: pl.BlockSpec((1, 1, d), lambda bi, i: (bi * N_MOD + k, 0, 0))
    return pl.pallas_call(
        functools.partial(_pool_kernel, tm=tm, n=n),
        grid=(b, n // tm),
        in_specs=[
            pl.BlockSpec((1, tm, d), lambda bi, i: (bi, i, 0)),
            pl.BlockSpec((1, POOL_HALO, d), lambda bi, i: (bi, jnp.maximum(i * hb - 1, 0), 0)),
            pl.BlockSpec((1, POOL_HALO, d), lambda bi, i: (bi, jnp.minimum((i + 1) * hb, n // POOL_HALO - 1), 0)),
            pl.BlockSpec((1, d), lambda bi, i: (0, 0)),
            mod(3), mod(4), mod(5),
            pl.BlockSpec(w.shape, lambda bi, i: (0, 0, 0)),
            pl.BlockSpec((1, d), lambda bi, i: (0, 0)),
        ],
        out_specs=pl.BlockSpec((1, tm, d), lambda bi, i: (bi, i, 0)),
        out_shape=jax.ShapeDtypeStruct((b, n, d), F32),
        scratch_shapes=[pltpu.VMEM((tm + 2 * POOL_HALO, d), F32)],
        compiler_params=_params(("parallel", "arbitrary"), 48),
        name="pool_mixer",
    )(x3, x3, x3, g, mods, mods, mods, w, ps)


def _rope_tables(n):
    rows = n // GRID_W
    row = jnp.arange(rows, dtype=F32)
    col = jnp.arange(GRID_W, dtype=F32)

    def cos_sin(rot_dim):
        nf = rot_dim // 4
        inv = ROPE_BASE ** (-jnp.arange(nf, dtype=F32) / nf)
        expand = lambda fn: jnp.concatenate([jnp.repeat(fn(row[:, None] * inv), GRID_W, axis=0),
                                             jnp.tile(fn(col[:, None] * inv), (rows, 1))], axis=-1)
        return expand(jnp.cos), expand(jnp.sin)

    ca, sa = cos_sin(A_HEAD_DIM)
    cb, sb = cos_sin(B_ROPE)
    zb = jnp.zeros_like(cb)
    zpad = jnp.zeros((n, LANES - B_ROPE), F32)
    return (jnp.concatenate([ca, ca], axis=-1), jnp.concatenate([-sa, sa], axis=-1),
            jnp.concatenate([cb, cb, zpad], axis=-1), jnp.concatenate([-sb, zb, zpad], axis=-1),
            jnp.concatenate([zb, sb, zpad], axis=-1))


def _identity_tables(n):
    one = jnp.ones((n, LANES), F32)
    zero = jnp.zeros((n, LANES), F32)
    return (one, zero, one, zero, zero)


def _attn_weights(w_in, w_uq, w_ukv):
    d = w_in.shape[0]
    win = jnp.concatenate([w_in, jnp.zeros((d, IN_W_PAD - IN_W), w_in.dtype)], axis=1).astype(BF16)
    uq = w_uq.reshape(B_Q_RANK, B_HEADS, B_QK)
    uq = jnp.concatenate([uq, jnp.zeros((B_Q_RANK, B_HEADS, B_PAD - B_QK), uq.dtype)], axis=-1)
    ukv = w_ukv.reshape(B_KV_RANK, B_HEADS, B_NOPE + B_V)
    uk = ukv[:, :, :B_NOPE].reshape(B_KV_RANK, B_HEADS * B_NOPE)
    uvt = ukv[:, :, B_NOPE:].reshape(B_KV_RANK, B_HEADS * B_V).T
    return win, uq.reshape(B_Q_RANK, B_HEADS * B_PAD).astype(BF16), uk.astype(BF16), uvt.astype(BF16)


def kernel(x, c, ctx, c_ctx, w_ada, b_ada, norm_ffn1, norm_mix, norm_ffn2, ffn1_w_gate, ffn1_w_up, ffn1_w_down,
           ffn2_w_gate, ffn2_w_up, ffn2_w_down, attn_w_in, attn_sink, mla_q_norm, mla_w_uq, mla_kv_norm, mla_w_ukv,
           attn_w_out, pool_w, pool_scale, final_norm):
    b, n, d = x.shape
    nc = ctx.shape[1]
    depth = w_ada.shape[0]
    f = ffn1_w_gate.shape[2]
    t = b * n

    tm = min(512, n)
    tm_ffn = min(1024, n)
    tf = min(512, f)
    tmc = min(512, b * nc)
    tq_a = min(256, n)
    tq_b = min(1024, n)
    assert n % tm == 0 and f % tf == 0 and (b * nc) % tmc == 0 and n % GRID_W == 0

    tiles_per_sample = n // tm
    x_row = lambda i: i // tiles_per_sample
    ffn_row = lambda i: i // (n // tm_ffn)
    ctx_row = lambda i: b

    mods = _mods(c, c_ctx, w_ada, b_ada)
    rope = _rope_tables(n)
    fn = final_norm.reshape(1, d)

    ffn_f32 = ((ffn1_w_gate, ffn1_w_up, ffn1_w_down), (ffn2_w_gate, ffn2_w_up, ffn2_w_down))
    ffn_bf16 = {}

    def ffn_weights(family, layer):
        if (family, layer) not in ffn_bf16:
            ffn_bf16[(family, layer)] = tuple(w[layer].astype(BF16) for w in ffn_f32[family])
        return ffn_bf16[(family, layer)]

    xs = x.reshape(t, d)
    cx = ctx.reshape(b * nc, d)
    for l in range(depth):
        is_attn = l % 2 == 0
        ctx_out = any(j % 2 == 0 for j in range(l + 1, depth))
        ctx_in = is_attn or ctx_out
        i = l // 2
        m = mods[l]
        g1, gm, g2 = (v[l].reshape(1, d) for v in (norm_ffn1, norm_mix, norm_ffn2))
        w1 = ffn_weights(0, l)

        xs = _ffn(xs, m, 0, ffn_row, g1, *w1, fn, tm=tm_ffn, tf=tf, final_norm=False)
        if ctx_in:
            cx = _ffn(cx, m, 0, ctx_row, g1, *w1, fn, tm=tmc, tf=tf, final_norm=False)

        if is_attn:
            win, wuq, wuk, wuvt = _attn_weights(attn_w_in[i], mla_w_uq[i], mla_w_ukv[i])
            qn = mla_q_norm[i].reshape(1, -1)
            kvn = mla_kv_norm[i].reshape(1, -1)
            qa, ka, va, qb, kb, vbt = _inproj(xs, m, x_row, gm, win, qn, wuq, kvn, wuk, wuvt, rope,
                                              lambda ti: ti % tiles_per_sample, tm=tm)
            _, kca, vca, _, kcb, vcbt = _inproj(cx, m, ctx_row, gm, win, qn, wuq, kvn, wuk, wuvt,
                                                _identity_tables(nc), lambda ti: 0, tm=nc)
            r3 = lambda a, rows: a.reshape(b, rows, a.shape[-1])
            r4 = lambda a: a.reshape(b, a.shape[0] // b, a.shape[1], a.shape[2])
            oa = _window_gqa(r3(qa, n), r3(ka, n), r3(va, n), r3(kca, nc), r3(vca, nc), attn_sink[i], tq=tq_a)
            pending = [(fam, lj) for lj in range(l, depth) for fam in (0, 1) if (lj, fam) > (l, 0)]
            ob, cast = _mla_attention(r3(qb, n), r3(kb, n), r4(vbt), r3(kcb, nc), r4(vcbt),
                                      [(w, lj) for fam, lj in pending for w in ffn_f32[fam]], tq=tq_b)
            for k, key in enumerate(pending):
                ffn_bf16[key] = tuple(cast[3 * k:3 * k + 3])
            xs = _outproj(xs, oa.reshape(t, -1), ob.reshape(t, -1), attn_w_out[i].astype(BF16), m, x_row, tm=tm)
            if ctx_out:
                raise NotImplementedError("context-stream attention output is not needed at this depth")
        else:
            xs = _pool_mixer(xs.reshape(b, n, d), m, gm, pool_w[i].astype(BF16), pool_scale[i].reshape(1, d),
                             tm=tm).reshape(t, d)
            if ctx_out:
                raise NotImplementedError("context-stream pooling output is not needed at this depth")

        w2 = ffn_weights(1, l)
        xs = _ffn(xs, m, 6, ffn_row, g2, *w2, fn, tm=tm_ffn, tf=tf, final_norm=(l == depth - 1))
        if ctx_out:
            cx = _ffn(cx, m, 6, ctx_row, g2, *w2, fn, tm=tmc, tf=tf, final_norm=False)
    return xs.reshape(b, n, d)
```

```python
import functools

import jax
import jax.numpy as jnp
from jax import lax
from jax.experimental import pallas as pl
from jax.experimental.pallas import tpu as pltpu

F32 = jnp.float32
BF16 = jnp.bfloat16

LANES = 128
SUBLANES = 8
BF16_SUBLANES = 16
MIB = 1 << 20

EPS = 1e-6
ROPE_BASE = 10000.0
GRID_W = 64
N_MOD = 9
A_HEADS = 8
A_KV_HEADS = 2
A_REP = A_HEADS // A_KV_HEADS
A_HEAD_DIM = 128
A_WINDOW = 128
B_HEADS = 8
B_NOPE = 128
B_ROPE = 64
B_QK = B_NOPE + B_ROPE
B_V = 128
B_Q_RANK = 768
B_KV_RANK = 256
B_PAD = 2 * LANES
B_VA = B_V + BF16_SUBLANES
A_Q_W = A_HEADS * A_HEAD_DIM
A_KV_W = A_KV_HEADS * A_HEAD_DIM
OFF_AK = A_Q_W
OFF_AV = OFF_AK + A_KV_W
OFF_CQ = OFF_AV + A_KV_W
OFF_CKV = OFF_CQ + B_Q_RANK
OFF_KR = OFF_CKV + B_KV_RANK
IN_W = OFF_KR + B_ROPE
IN_W_PAD = OFF_KR + LANES
POOL_WINDOWS = (2, 4, 8, 16)
POOL_HALO = SUBLANES
NEG_BIG = -1e30
LOG2_E = 1.4426950408889634
FFN_FIRST_STEP_ROWS = 256
INPROJ_ROWS = 512
MLA_GROUP = 4

MOD_ROWS = 8


def _params(semantics, vmem_mib):
    return pltpu.CompilerParams(dimension_semantics=semantics, vmem_limit_bytes=vmem_mib * MIB)


def _const_spec(shape):
    nd = len(shape)
    return pl.BlockSpec(shape, lambda *_: (0,) * nd, pipeline_mode=pl.Buffered(1))


def _rms_mod(x, g, shift, scale):
    y = x * lax.rsqrt(jnp.mean(x * x, axis=-1, keepdims=True) + EPS)
    return y * (g * (1.0 + scale)) + shift


def _dot(a, b):
    return jnp.dot(a, b, preferred_element_type=F32)


def _dot_nt(a, b):
    return lax.dot_general(a, b, (((1,), (1,)), ((), ())), preferred_element_type=F32)


def _dot_tn(a, b):
    return lax.dot_general(a, b, (((0,), (0,)), ((), ())), preferred_element_type=F32)


def _mods_kernel(c_ref, w_ref, b_ref, o_ref):
    c = c_ref[...]
    a = (c * jax.nn.sigmoid(c)).astype(BF16)
    o_ref[0] = _dot(a, w_ref[0].astype(BF16)) + b_ref[0]


def _mods(c, c_ctx, w_ada, b_ada):
    depth, d, nd = w_ada.shape
    b = c.shape[0]
    assert b + 1 <= MOD_ROWS
    cp = jnp.concatenate([c, c_ctx[None, :], jnp.zeros((MOD_ROWS - b - 1, d), F32)], axis=0)
    tn = min(d, 1024)
    assert nd % tn == 0
    out = pl.pallas_call(
        _mods_kernel,
        grid=(depth, nd // tn),
        in_specs=[
            pl.BlockSpec((MOD_ROWS, d), lambda l, j: (0, 0)),
            pl.BlockSpec((1, d, tn), lambda l, j: (l, 0, j)),
            pl.BlockSpec((1, 1, tn), lambda l, j: (l, 0, j)),
        ],
        out_specs=pl.BlockSpec((1, MOD_ROWS, tn), lambda l, j: (l, 0, j)),
        out_shape=jax.ShapeDtypeStruct((depth, MOD_ROWS, nd), F32),
        compiler_params=_params(("arbitrary", "arbitrary"), 40),
        name="adaln_mods",
    )(cp, w_ada, b_ada.reshape(depth, 1, nd))
    return out.reshape(depth, MOD_ROWS * N_MOD, 1, d)


def _mod_spec(d, row_of_tile, k):
    return pl.BlockSpec((1, 1, d), lambda i, *_: (row_of_tile(i) * N_MOD + k, 0, 0))


def _ffn_kernel(x_ref, g_ref, sh_ref, sc_ref, gt_ref, wg_ref, wu_ref, wd_ref, fn_ref,
                o_ref, h_ref, *, final_norm):
    j = pl.program_id(1)

    def accumulate(h, base):
        gate = _dot(h, wg_ref[...])
        up = _dot(h, wu_ref[...])
        a = (gate * jax.nn.sigmoid(gate)) * up
        return base + (0.5 * gt_ref[0]) * _dot(a.astype(BF16), wd_ref[...])

    @pl.when(j == 0)
    def _():
        gain = g_ref[...] * (1.0 + sc_ref[0])
        shift = sh_ref[0]
        rb = min(FFN_FIRST_STEP_ROWS, x_ref.shape[0])
        for r0 in range(0, x_ref.shape[0], rb):
            rows = slice(r0, r0 + rb)
            x = x_ref[rows, :]
            h = ((x * lax.rsqrt(jnp.mean(x * x, axis=-1, keepdims=True) + EPS)) * gain + shift).astype(BF16)
            h_ref[rows, :] = h
            o_ref[rows, :] = accumulate(h, x_ref[rows, :])

    @pl.when(j > 0)
    def _():
        o_ref[...] = accumulate(h_ref[...], o_ref[...])

    if final_norm:
        @pl.when(j == pl.num_programs(1) - 1)
        def _():
            out = o_ref[...]
            o_ref[...] = out * lax.rsqrt(jnp.mean(out * out, axis=-1, keepdims=True) + EPS) * fn_ref[...]


def _ffn(x, mods, mod_base, row_of_tile, g, wg, wu, wd, fn, *, tm, tf, final_norm):
    t, d = x.shape
    f = wg.shape[1]
    return pl.pallas_call(
        functools.partial(_ffn_kernel, final_norm=final_norm),
        grid=(t // tm, f // tf),
        in_specs=[
            pl.BlockSpec((tm, d), lambda i, j: (i, 0)),
            pl.BlockSpec((1, d), lambda i, j: (0, 0)),
            _mod_spec(d, row_of_tile, mod_base),
            _mod_spec(d, row_of_tile, mod_base + 1),
            _mod_spec(d, row_of_tile, mod_base + 2),
            pl.BlockSpec((d, tf), lambda i, j: (0, j)),
            pl.BlockSpec((d, tf), lambda i, j: (0, j)),
            pl.BlockSpec((tf, d), lambda i, j: (j, 0)),
            pl.BlockSpec((1, d), lambda i, j: (0, 0)),
        ],
        out_specs=pl.BlockSpec((tm, d), lambda i, j: (i, 0)),
        out_shape=jax.ShapeDtypeStruct((t, d), F32),
        scratch_shapes=[pltpu.VMEM((tm, d), BF16)],
        compiler_params=_params(("parallel", "arbitrary"), 63),
        name="swiglu_half_step",
    )(x, g, mods, mods, mods, wg, wu, wd, fn)


def _rope_a(x, cos, sin):
    return x * cos + pltpu.roll(x, A_HEAD_DIM // 2, 1) * sin


def _rope_b(x, cos, sin_lo, sin_hi):
    half = B_ROPE // 2
    return x * cos + pltpu.roll(x, LANES - half, 1) * sin_lo + pltpu.roll(x, half, 1) * sin_hi


def _inproj_kernel(x_ref, g_ref, sh_ref, sc_ref, win_ref, wtail_ref, qn_ref, wuq_ref, kvn_ref, wuk_ref, wuvt_ref,
                   cosa_ref, sina_ref, cosb_ref, sinlo_ref, sinhi_ref,
                   qa_ref, ka_ref, va_ref, qb_ref, kb_ref, vbt_ref):
    rb = min(INPROJ_ROWS, x_ref.shape[0])
    for r0 in range(0, x_ref.shape[0], rb):
        rows = slice(r0, r0 + rb)
        h = _rms_mod(x_ref[rows, :], g_ref[...], sh_ref[0], sc_ref[0]).astype(BF16)
        cosa, sina = cosa_ref[rows, :], sina_ref[rows, :]
        cosb, sinlo, sinhi = cosb_ref[rows, :], sinlo_ref[rows, :], sinhi_ref[rows, :]

        aq = _dot(h, win_ref[:, 0:OFF_AK]) * (A_HEAD_DIM ** -0.5 * LOG2_E)
        for s in range(A_HEADS):
            sl = slice(s * LANES, (s + 1) * LANES)
            qa_ref[rows, sl] = _rope_a(aq[:, sl], cosa, sina).astype(BF16)
        akv = _dot(h, win_ref[:, OFF_AK:OFF_CQ])
        for s in range(A_KV_HEADS):
            sl = slice(s * LANES, (s + 1) * LANES)
            ka_ref[rows, sl] = _rope_a(akv[:, sl], cosa, sina).astype(BF16)
        va_ref[rows, :] = akv[:, A_KV_W:].astype(BF16)

        cq = _dot(h, win_ref[:, OFF_CQ:OFF_CKV])
        cqn = (cq * lax.rsqrt(jnp.mean(cq * cq, axis=-1, keepdims=True) + EPS) * qn_ref[...]).astype(BF16)
        qb = _dot(cqn, wuq_ref[...]) * (B_QK ** -0.5 * LOG2_E)
        for hd in range(B_HEADS):
            lo = hd * B_PAD
            qb_ref[rows, lo:lo + LANES] = qb[:, lo:lo + LANES].astype(BF16)
            qb_ref[rows, lo + LANES:lo + B_PAD] = _rope_b(qb[:, lo + LANES:lo + B_PAD], cosb, sinlo,
                                                          sinhi).astype(BF16)

        ckr = _dot(h, wtail_ref[...])
        ckv = ckr[:, 0:B_KV_RANK]
        ckvn = (ckv * lax.rsqrt(jnp.mean(ckv * ckv, axis=-1, keepdims=True) + EPS) * kvn_ref[...]).astype(BF16)
        kr = _rope_b(ckr[:, B_KV_RANK:], cosb, sinlo, sinhi).astype(BF16)
        kn = _dot(ckvn, wuk_ref[...])
        for hd in range(B_HEADS):
            lo = hd * B_PAD
            kb_ref[rows, lo:lo + LANES] = kn[:, hd * B_NOPE:(hd + 1) * B_NOPE].astype(BF16)
            kb_ref[rows, lo + LANES:lo + B_PAD] = kr
        vt = _dot_nt(wuvt_ref[...], ckvn).astype(BF16)
        ones_row = (lax.broadcasted_iota(jnp.int32, (B_VA - B_V, rb), 0) == 0).astype(F32).astype(BF16)
        for hd in range(B_HEADS):
            vbt_ref[0, hd * B_VA:hd * B_VA + B_V, rows] = vt[hd * B_V:(hd + 1) * B_V]
            vbt_ref[0, hd * B_VA + B_V:(hd + 1) * B_VA, rows] = ones_row


def _inproj(x, mods, row_of_tile, g, weights, tables, table_tile, *, tm):
    t, d = x.shape
    tok = lambda w: pl.BlockSpec((tm, w), lambda i: (i, 0))
    tab = pl.BlockSpec((tm, LANES), lambda i: (table_tile(i), 0))
    widths = (A_Q_W, A_KV_W, A_KV_W, B_HEADS * B_PAD, B_HEADS * B_PAD)
    vt_rows = B_HEADS * B_VA
    return pl.pallas_call(
        _inproj_kernel,
        grid=(t // tm,),
        in_specs=[
            tok(d),
            _const_spec((1, d)),
            _mod_spec(d, row_of_tile, 3),
            _mod_spec(d, row_of_tile, 4),
            *[_const_spec(w.shape) for w in weights],
            tab, tab, tab, tab, tab,
        ],
        out_specs=[tok(w) for w in widths] + [pl.BlockSpec((1, vt_rows, tm), lambda i: (i, 0, 0))],
        out_shape=[jax.ShapeDtypeStruct((t, w), BF16) for w in widths]
        + [jax.ShapeDtypeStruct((t // tm, vt_rows, tm), BF16)],
        compiler_params=_params(("parallel",), 56),
        name="attn_in_proj",
    )(x, g, mods, mods, *weights, *tables)


def _window_kernel(sink_ref, q_ref, k_ref, v_ref, kc_ref, vc_ref, o_ref, qt_ref, *, tq, n):
    qi = pl.program_id(1)
    span = tq + 2 * A_WINDOW
    start = pl.multiple_of(jnp.clip(qi * tq - A_WINDOW, 0, n - span), A_WINDOW)
    kpos = start + lax.broadcasted_iota(jnp.int32, (span, tq), 0)
    qpos = qi * tq + lax.broadcasted_iota(jnp.int32, (span, tq), 1)
    nc = kc_ref.shape[1]
    bias = jnp.concatenate([jnp.where(jnp.abs(kpos - qpos) <= A_WINDOW, 0.0, NEG_BIG), jnp.zeros((nc, tq), F32)],
                           axis=0)
    for g in range(A_KV_HEADS):
        gl = slice(g * A_HEAD_DIM, (g + 1) * A_HEAD_DIM)
        heads = [slice((g * A_REP + r) * A_HEAD_DIM, (g * A_REP + r + 1) * A_HEAD_DIM) for r in range(A_REP)]
        k = jnp.concatenate([k_ref[0, pl.ds(start, span), gl], kc_ref[0, :, gl]], axis=0)
        v = jnp.concatenate([v_ref[0, pl.ds(start, span), gl], vc_ref[0, :, gl]], axis=0)
        for r, hl in enumerate(heads):
            qt_ref[:, r * tq:(r + 1) * tq] = q_ref[0, :, hl].T
        s = _dot(k, qt_ref[...])
        s = jnp.concatenate([s[:, r * tq:(r + 1) * tq] + bias for r in range(A_REP)], axis=1)
        sink = jnp.concatenate([jnp.full((1, tq), sink_ref[g * A_REP + r] * LOG2_E, F32) for r in range(A_REP)],
                               axis=1)
        m = jnp.maximum(jnp.max(s, axis=0, keepdims=True), sink)
        p = jnp.exp2(s - m)
        denom = jnp.sum(p, axis=0, keepdims=True) + jnp.exp2(sink - m)
        o = jnp.transpose(_dot_tn(v, p.astype(BF16)) / denom)
        for r, hl in enumerate(heads):
            o_ref[0, :, hl] = o[r * tq:(r + 1) * tq].astype(BF16)


def _window_gqa(qa, ka, va, kca, vca, sink, *, tq):
    b, n, _ = qa.shape
    nc = kca.shape[1]
    assert n >= tq + 2 * A_WINDOW
    return pl.pallas_call(
        functools.partial(_window_kernel, tq=tq, n=n),
        grid=(b, n // tq),
        in_specs=[
            pl.BlockSpec(memory_space=pltpu.SMEM),
            pl.BlockSpec((1, tq, A_Q_W), lambda bi, qi: (bi, qi, 0)),
            pl.BlockSpec((1, n, A_KV_W), lambda bi, qi: (bi, 0, 0)),
            pl.BlockSpec((1, n, A_KV_W), lambda bi, qi: (bi, 0, 0)),
            pl.BlockSpec((1, nc, A_KV_W), lambda bi, qi: (bi, 0, 0)),
            pl.BlockSpec((1, nc, A_KV_W), lambda bi, qi: (bi, 0, 0)),
        ],
        out_specs=pl.BlockSpec((1, tq, A_Q_W), lambda bi, qi: (bi, qi, 0)),
        out_shape=jax.ShapeDtypeStruct((b, n, A_Q_W), BF16),
        scratch_shapes=[pltpu.VMEM((A_HEAD_DIM, A_REP * tq), BF16)],
        compiler_params=_params(("parallel", "arbitrary"), 48),
        name="window_gqa",
    )(sink, qa, ka, va, kca, vca)


def _mla_kernel(q_ref, k_ref, vt_ref, kc_ref, vct_ref, *rest, tk, n_cast):
    cast_src, o_ref, cast_dst = rest[:n_cast], rest[n_cast], rest[n_cast + 1:2 * n_cast + 1]
    sa_ref, sb_ref, qt_ref = rest[2 * n_cast + 1:]
    for src, dst in zip(cast_src, cast_dst):
        dst[...] = src[...].astype(BF16)

    qt_ref[...] = q_ref[0].T
    qt = qt_ref[...]
    nchunks = vt_ref.shape[1]

    def stage_scores(buf, c):
        s = _dot(k_ref[0, pl.ds(pl.multiple_of(c * tk, tk), tk), :], qt)
        buf[...] = s
        return jnp.max(s, axis=0, keepdims=True)

    def probs(s, m):
        return jnp.exp2(s - m).astype(BF16)

    def update(s, s_max, vt, m_old, acc_old):
        m = jnp.maximum(m_old, s_max)
        acc = jnp.exp2(m_old - m) * acc_old + _dot(vt, probs(s, m))
        return m, acc

    bufs = (sa_ref, sb_ref)
    next_max = stage_scores(bufs[0], 0)
    s = _dot(kc_ref[0], qt)
    m = jnp.max(s, axis=0, keepdims=True)
    carry = (m, _dot(vct_ref[0, 0], probs(s, m)), next_max)

    def group(c0, carry, last):
        m, acc, cur_max = carry
        for j in range(MLA_GROUP):
            next_max = cur_max
            if not (last and j == MLA_GROUP - 1):
                next_max = stage_scores(bufs[(j + 1) % 2], c0 + j + 1)
            m, acc = update(bufs[j % 2][...], cur_max, vt_ref[0, c0 + j], m, acc)
            cur_max = next_max
        return m, acc, cur_max

    carry = lax.fori_loop(0, nchunks // MLA_GROUP - 1,
                          lambda i, cr: group(i * MLA_GROUP, cr, False), carry)
    _, acc, _ = group(nchunks - MLA_GROUP, carry, True)
    o_ref[0] = jnp.transpose(acc[0:B_V] / acc[B_V:B_V + 1]).astype(BF16)


def _cast_split(rows, cols, steps):
    for cb in range(1, steps + 1):
        rb = steps // cb
        if (steps % cb == 0 and rows % rb == 0 and cols % cb == 0
                and (rows // rb) % BF16_SUBLANES == 0 and (cols // cb) % LANES == 0):
            return rb, cb
    raise ValueError(f"cannot split a ({rows}, {cols}) matrix into {steps} aligned blocks")


def _mla_attention(qb, kb, vbt, kcb, vcbt, casts, *, tq):
    b, n, _ = qb.shape
    nc = kcb.shape[1]
    nchunks, _, tk = vbt.shape[1:]
    nq = n // tq
    assert nchunks * tk == n and nchunks % MLA_GROUP == 0 and vcbt.shape[1] == 1 and vcbt.shape[3] == nc
    steps = b * B_HEADS * nq

    cast_in, cast_out, cast_shapes = [], [], []
    for w, layer in casts:
        _, rows, cols = w.shape
        rb, cb = _cast_split(rows, cols, steps)
        blk = (rows // rb, cols // cb)
        step = lambda bi, hi, qi: (bi * B_HEADS + hi) * nq + qi
        cast_in.append(pl.BlockSpec((None,) + blk,
                                    lambda bi, hi, qi, layer=layer, cb=cb: (layer, step(bi, hi, qi) // cb,
                                                                            step(bi, hi, qi) % cb)))
        cast_out.append(pl.BlockSpec(blk, lambda bi, hi, qi, cb=cb: (step(bi, hi, qi) // cb, step(bi, hi, qi) % cb)))
        cast_shapes.append(jax.ShapeDtypeStruct((rows, cols), BF16))

    out = pl.pallas_call(
        functools.partial(_mla_kernel, tk=tk, n_cast=len(casts)),
        grid=(b, B_HEADS, nq),
        in_specs=[
            pl.BlockSpec((1, tq, B_PAD), lambda bi, hi, qi: (bi, qi, hi)),
            pl.BlockSpec((1, n, B_PAD), lambda bi, hi, qi: (bi, 0, hi)),
            pl.BlockSpec((1, nchunks, B_VA, tk), lambda bi, hi, qi: (bi, 0, hi, 0)),
            pl.BlockSpec((1, nc, B_PAD), lambda bi, hi, qi: (bi, 0, hi)),
            pl.BlockSpec((1, 1, B_VA, nc), lambda bi, hi, qi: (bi, 0, hi, 0)),
        ] + cast_in,
        out_specs=[pl.BlockSpec((1, tq, B_V), lambda bi, hi, qi: (bi, qi, hi))] + cast_out,
        out_shape=[jax.ShapeDtypeStruct((b, n, B_HEADS * B_V), BF16)] + cast_shapes,
        scratch_shapes=[pltpu.VMEM((tk, tq), F32), pltpu.VMEM((tk, tq), F32), pltpu.VMEM((B_PAD, tq), BF16)],
        compiler_params=_params(("arbitrary", "arbitrary", "arbitrary"), 48),
        name="mla_attention",
    )(qb, kb, vbt, kcb, vcbt, *[w for w, _ in casts])
    return out[0], out[1:]


def _outproj_kernel(x_ref, oa_ref, ob_ref, w_ref, gt_ref, o_ref):
    ka = oa_ref.shape[1]
    y = _dot(oa_ref[...], w_ref[0:ka, :]) + _dot(ob_ref[...], w_ref[ka:, :])
    o_ref[...] = x_ref[...] + gt_ref[0] * y


def _outproj(x, oa, ob, w, mods, row_of_tile, *, tm):
    t, d = x.shape
    return pl.pallas_call(
        _outproj_kernel,
        grid=(t // tm,),
        in_specs=[
            pl.BlockSpec((tm, d), lambda i: (i, 0)),
            pl.BlockSpec((tm, oa.shape[1]), lambda i: (i, 0)),
            pl.BlockSpec((tm, ob.shape[1]), lambda i: (i, 0)),
            _const_spec(w.shape),
            _mod_spec(d, row_of_tile, 5),
        ],
        out_specs=pl.BlockSpec((tm, d), lambda i: (i, 0)),
        out_shape=jax.ShapeDtypeStruct((t, d), F32),
        compiler_params=_params(("parallel",), 48),
        name="attn_out_proj",
    )(x, oa, ob, w, mods)


def _pool_kernel(x_ref, xp_ref, xn_ref, g_ref, sh_ref, sc_ref, gt_ref, w_ref, ps_ref, o_ref, hp_ref,
                 *, tm, n):
    i = pl.program_id(1)
    last = pl.num_programs(1) - 1
    g, sh, sc = g_ref[...], sh_ref[0], sc_ref[0]
    x = x_ref[0]
    hp_ref[0:POOL_HALO, :] = jnp.where(i > 0, _rms_mod(xp_ref[0], g, sh, sc), 0.0)
    hp_ref[POOL_HALO:POOL_HALO + tm, :] = _rms_mod(x, g, sh, sc)
    hp_ref[POOL_HALO + tm:, :] = jnp.where(i < last, _rms_mod(xn_ref[0], g, sh, sc), 0.0)

    t = i * tm + lax.broadcasted_iota(jnp.int32, (tm, 1), 0)
    gw = x.shape[1] // len(POOL_WINDOWS)
    for gi, w in enumerate(POOL_WINDOWS):
        cols = slice(gi * gw, (gi + 1) * gw)
        rows = tm + 2 * POOL_HALO
        shift_up = lambda a, k: a if k == 0 else pltpu.roll(a, rows - k, 0)
        tot = shift_up(hp_ref[:, cols], POOL_HALO - w // 2)
        k = 1
        while k < w:
            tot = tot + shift_up(tot, k)
            k *= 2
        tot = tot[0:tm]
        cnt =(jnp.minimum(t - w // 2 + w, n) - jnp.maximum(t - w // 2, 0)).astype(F32)
        diff = tot / cnt - hp_ref[pl.ds(POOL_HALO, tm), cols]
        y = _dot(diff.astype(BF16), w_ref[gi])
        o_ref[0, :, cols] = x[:, cols] + (gt_ref[0][:, cols] * ps_ref[:, cols]) * y


def _pool_mixer(x3, mods, g, w, ps, *, tm):
    b, n, d = x3.shape
    hb = tm // POOL_HALO
    mod = lambda k: pl.BlockSpec((1, 1, d), lambda bi, i: (bi * N_MOD + k, 0, 0))
    return pl.pallas_call(
        functools.partial(_pool_kernel, tm=tm, n=n),
        grid=(b, n // tm),
        in_specs=[
            pl.BlockSpec((1, tm, d), lambda bi, i: (bi, i, 0)),
            pl.BlockSpec((1, POOL_HALO, d), lambda bi, i: (bi, jnp.maximum(i * hb - 1, 0), 0)),
            pl.BlockSpec((1, POOL_HALO, d), lambda bi, i: (bi, jnp.minimum((i + 1) * hb, n // POOL_HALO - 1), 0)),
            pl.BlockSpec((1, d), lambda bi, i: (0, 0)),
            mod(3), mod(4), mod(5),
            pl.BlockSpec(w.shape, lambda bi, i: (0, 0, 0)),
            pl.BlockSpec((1, d), lambda bi, i: (0, 0)),
        ],
        out_specs=pl.BlockSpec((1, tm, d), lambda bi, i: (bi, i, 0)),
        out_shape=jax.ShapeDtypeStruct((b, n, d), F32),
        scratch_shapes=[pltpu.VMEM((tm + 2 * POOL_HALO, d), F32)],
        compiler_params=_params(("parallel", "arbitrary"), 48),
        name="pool_mixer",
    )(x3, x3, x3, g, mods, mods, mods, w, ps)


def _rope_tables(n):
    rows = n // GRID_W
    row = jnp.arange(rows, dtype=F32)
    col = jnp.arange(GRID_W, dtype=F32)

    def cos_sin(rot_dim):
        nf = rot_dim // 4
        inv = ROPE_BASE ** (-jnp.arange(nf, dtype=F32) / nf)
        expand = lambda fn: jnp.concatenate([jnp.repeat(fn(row[:, None] * inv), GRID_W, axis=0),
                                             jnp.tile(fn(col[:, None] * inv), (rows, 1))], axis=-1)
        return expand(jnp.cos), expand(jnp.sin)

    ca, sa = cos_sin(A_HEAD_DIM)
    cb, sb = cos_sin(B_ROPE)
    zb = jnp.zeros_like(cb)
    zpad = jnp.zeros((n, LANES - B_ROPE), F32)
    return (jnp.concatenate([ca, ca], axis=-1), jnp.concatenate([-sa, sa], axis=-1),
            jnp.concatenate([cb, cb, zpad], axis=-1), jnp.concatenate([-sb, zb, zpad], axis=-1),
            jnp.concatenate([zb, sb, zpad], axis=-1))


def _identity_tables(n):
    one = jnp.ones((n, LANES), F32)
    zero = jnp.zeros((n, LANES), F32)
    return (one, zero, one, zero, zero)


def _attn_weights(w_in, q_norm, w_uq, kv_norm, w_ukv):
    d = w_in.shape[0]
    head = w_in[:, :OFF_CKV].astype(BF16)
    tail = jnp.concatenate([w_in[:, OFF_CKV:], jnp.zeros((d, IN_W_PAD - IN_W), w_in.dtype)], axis=1).astype(BF16)
    uq = w_uq.reshape(B_Q_RANK, B_HEADS, B_QK)
    uq = jnp.concatenate([uq, jnp.zeros((B_Q_RANK, B_HEADS, B_PAD - B_QK), uq.dtype)], axis=-1)
    ukv = w_ukv.reshape(B_KV_RANK, B_HEADS, B_NOPE + B_V)
    uk = ukv[:, :, :B_NOPE].reshape(B_KV_RANK, B_HEADS * B_NOPE)
    uvt = ukv[:, :, B_NOPE:].reshape(B_KV_RANK, B_HEADS * B_V).T
    return (head, tail, q_norm.reshape(1, -1), uq.reshape(B_Q_RANK, B_HEADS * B_PAD).astype(BF16),
            kv_norm.reshape(1, -1), uk.astype(BF16), uvt.astype(BF16))


def kernel(x, c, ctx, c_ctx, w_ada, b_ada, norm_ffn1, norm_mix, norm_ffn2, ffn1_w_gate, ffn1_w_up, ffn1_w_down,
           ffn2_w_gate, ffn2_w_up, ffn2_w_down, attn_w_in, attn_sink, mla_q_norm, mla_w_uq, mla_kv_norm, mla_w_ukv,
           attn_w_out, pool_w, pool_scale, final_norm):
    b, n, d = x.shape
    nc = ctx.shape[1]
    depth = w_ada.shape[0]
    f = ffn1_w_gate.shape[2]
    t = b * n

    tm = min(512, n)
    tm_ffn = min(1024, n)
    tf = min(512, f)
    tmc = min(512, b * nc)
    tq_a = min(256, n)
    tq_b = min(1024, n)
    assert n % tm == 0 and f % tf == 0 and (b * nc) % tmc == 0 and n % GRID_W == 0

    tiles_per_sample = n // tm
    x_row = lambda i: i // tiles_per_sample
    ffn_row = lambda i: i // (n // tm_ffn)
    ctx_row = lambda i: b

    mods = _mods(c, c_ctx, w_ada, b_ada)
    rope = _rope_tables(n)
    fn = final_norm.reshape(1, d)

    ffn_f32 = ((ffn1_w_gate, ffn1_w_up, ffn1_w_down), (ffn2_w_gate, ffn2_w_up, ffn2_w_down))
    ffn_bf16 = {}

    def ffn_weights(family, layer):
        if (family, layer) not in ffn_bf16:
            ffn_bf16[(family, layer)] = tuple(w[layer].astype(BF16) for w in ffn_f32[family])
        return ffn_bf16[(family, layer)]

    xs = x.reshape(t, d)
    cx = ctx.reshape(b * nc, d)
    for l in range(depth):
        is_attn = l % 2 == 0
        ctx_out = any(j % 2 == 0 for j in range(l + 1, depth))
        ctx_in = is_attn or ctx_out
        i = l // 2
        m = mods[l]
        g1, gm, g2 = (v[l].reshape(1, d) for v in (norm_ffn1, norm_mix, norm_ffn2))
        w1 = ffn_weights(0, l)

        xs = _ffn(xs, m, 0, ffn_row, g1, *w1, fn, tm=tm_ffn, tf=tf, final_norm=False)
        if ctx_in:
            cx = _ffn(cx, m, 0, ctx_row, g1, *w1, fn, tm=tmc, tf=tf, final_norm=False)

        if is_attn:
            aw = _attn_weights(attn_w_in[i], mla_q_norm[i], mla_w_uq[i], mla_kv_norm[i], mla_w_ukv[i])
            qa, ka, va, qb, kb, vbt = _inproj(xs, m, x_row, gm, aw, rope, lambda ti: ti % tiles_per_sample, tm=tm)
            _, kca, vca, _, kcb, vcbt = _inproj(cx, m, ctx_row, gm, aw, _identity_tables(nc), lambda ti: 0, tm=nc)
            r3 = lambda a, rows: a.reshape(b, rows, a.shape[-1])
            r4 = lambda a: a.reshape(b, a.shape[0] // b, a.shape[1], a.shape[2])
            oa = _window_gqa(r3(qa, n), r3(ka, n), r3(va, n), r3(kca, nc), r3(vca, nc), attn_sink[i], tq=tq_a)
            pending = [(fam, lj) for lj in range(l, depth) for fam in (0, 1) if (lj, fam) > (l, 0)]
            ob, cast = _mla_attention(r3(qb, n), r3(kb, n), r4(vbt), r3(kcb, nc), r4(vcbt),
                                      [(w, lj) for fam, lj in pending for w in ffn_f32[fam]] + [(attn_w_out, i)],
                                      tq=tq_b)
            for k, key in enumerate(pending):
                ffn_bf16[key] = tuple(cast[3 * k:3 * k + 3])
            xs = _outproj(xs, oa.reshape(t, -1), ob.reshape(t, -1), cast[-1], m, x_row, tm=tm)
            if ctx_out:
                raise NotImplementedError("context-stream attention output is not needed at this depth")
        else:
            xs = _pool_mixer(xs.reshape(b, n, d), m, gm, pool_w[i].astype(BF16), pool_scale[i].reshape(1, d),
                             tm=tm).reshape(t, d)
            if ctx_out:
                raise NotImplementedError("context-stream pooling output is not needed at this depth")

        w2 = ffn_weights(1, l)
        xs = _ffn(xs, m, 6, ffn_row, g2, *w2, fn, tm=tm_ffn, tf=tf, final_norm=(l == depth - 1))
        if ctx_out:
            cx = _ffn(cx, m, 6, ctx_row, g2, *w2, fn, tm=tmc, tf=tf, final_norm=False)
    return xs.reshape(b, n, d)
```

```python
import functools

import jax
import jax.numpy as jnp
from jax import lax
from jax.experimental import pallas as pl
from jax.experimental.pallas import tpu as pltpu

F32 = jnp.float32
BF16 = jnp.bfloat16

LANES = 128
SUBLANES = 8
BF16_SUBLANES = 16
MIB = 1 << 20

EPS = 1e-6
ROPE_BASE = 10000.0
GRID_W = 64
N_MOD = 9
A_HEADS = 8
A_KV_HEADS = 2
A_REP = A_HEADS // A_KV_HEADS
A_HEAD_DIM = 128
A_WINDOW = 128
B_HEADS = 8
B_NOPE = 128
B_ROPE = 64
B_QK = B_NOPE + B_ROPE
B_V = 128
B_Q_RANK = 768
B_KV_RANK = 256
B_PAD = 2 * LANES
B_VA = B_V + BF16_SUBLANES
A_Q_W = A_HEADS * A_HEAD_DIM
A_KV_W = A_KV_HEADS * A_HEAD_DIM
OFF_AK = A_Q_W
OFF_AV = OFF_AK + A_KV_W
OFF_CQ = OFF_AV + A_KV_W
OFF_CKV = OFF_CQ + B_Q_RANK
OFF_KR = OFF_CKV + B_KV_RANK
IN_W = OFF_KR + B_ROPE
IN_W_PAD = OFF_KR + LANES
POOL_WINDOWS = (2, 4, 8, 16)
POOL_HALO = SUBLANES
NEG_BIG = -1e30
LOG2_E = 1.4426950408889634
FFN_FIRST_STEP_ROWS = 256
INPROJ_ROWS = 512
MLA_SUBTILES = 2
MLA_GROUP = 4

MOD_ROWS = 8


def _params(semantics, vmem_mib):
    return pltpu.CompilerParams(dimension_semantics=semantics, vmem_limit_bytes=vmem_mib * MIB)


def _const_spec(shape):
    nd = len(shape)
    return pl.BlockSpec(shape, lambda *_: (0,) * nd, pipeline_mode=pl.Buffered(1))


def _rms_mod(x, g, shift, scale):
    y = x * lax.rsqrt(jnp.mean(x * x, axis=-1, keepdims=True) + EPS)
    return y * (g * (1.0 + scale)) + shift


def _dot(a, b):
    return jnp.dot(a, b, preferred_element_type=F32)


def _dot_nt(a, b):
    return lax.dot_general(a, b, (((1,), (1,)), ((), ())), preferred_element_type=F32)


def _dot_tn(a, b):
    return lax.dot_general(a, b, (((0,), (0,)), ((), ())), preferred_element_type=F32)


def _mods_kernel(c_ref, w_ref, b_ref, o_ref):
    c = c_ref[...]
    a = (c * jax.nn.sigmoid(c)).astype(BF16)
    o_ref[0] = _dot(a, w_ref[0].astype(BF16)) + b_ref[0]


def _mods(c, c_ctx, w_ada, b_ada):
    depth, d, nd = w_ada.shape
    b = c.shape[0]
    assert b + 1 <= MOD_ROWS
    cp = jnp.concatenate([c, c_ctx[None, :], jnp.zeros((MOD_ROWS - b - 1, d), F32)], axis=0)
    tn = min(d, 1024)
    assert nd % tn == 0
    out = pl.pallas_call(
        _mods_kernel,
        grid=(depth, nd // tn),
        in_specs=[
            pl.BlockSpec((MOD_ROWS, d), lambda l, j: (0, 0)),
            pl.BlockSpec((1, d, tn), lambda l, j: (l, 0, j)),
            pl.BlockSpec((1, 1, tn), lambda l, j: (l, 0, j)),
        ],
        out_specs=pl.BlockSpec((1, MOD_ROWS, tn), lambda l, j: (l, 0, j)),
        out_shape=jax.ShapeDtypeStruct((depth, MOD_ROWS, nd), F32),
        compiler_params=_params(("arbitrary", "arbitrary"), 40),
        name="adaln_mods",
    )(cp, w_ada, b_ada.reshape(depth, 1, nd))
    return out.reshape(depth, MOD_ROWS * N_MOD, 1, d)


def _mod_spec(d, row_of_tile, k):
    return pl.BlockSpec((1, 1, d), lambda i, *_: (row_of_tile(i) * N_MOD + k, 0, 0))


def _ffn_kernel(x_ref, g_ref, sh_ref, sc_ref, gt_ref, wg_ref, wu_ref, wd_ref, fn_ref,
                o_ref, h_ref, *, final_norm):
    j = pl.program_id(1)

    def accumulate(h, base):
        gate = _dot(h, wg_ref[...])
        up = _dot(h, wu_ref[...])
        a = (gate * jax.nn.sigmoid(gate)) * up
        return base + (0.5 * gt_ref[0]) * _dot(a.astype(BF16), wd_ref[...])

    @pl.when(j == 0)
    def _():
        gain = g_ref[...] * (1.0 + sc_ref[0])
        shift = sh_ref[0]
        rb = min(FFN_FIRST_STEP_ROWS, x_ref.shape[0])
        for r0 in range(0, x_ref.shape[0], rb):
            rows = slice(r0, r0 + rb)
            x = x_ref[rows, :]
            h = ((x * lax.rsqrt(jnp.mean(x * x, axis=-1, keepdims=True) + EPS)) * gain + shift).astype(BF16)
            h_ref[rows, :] = h
            o_ref[rows, :] = accumulate(h, x_ref[rows, :])

    @pl.when(j > 0)
    def _():
        o_ref[...] = accumulate(h_ref[...], o_ref[...])

    if final_norm:
        @pl.when(j == pl.num_programs(1) - 1)
        def _():
            out = o_ref[...]
            o_ref[...] = out * lax.rsqrt(jnp.mean(out * out, axis=-1, keepdims=True) + EPS) * fn_ref[...]


def _ffn(x, mods, mod_base, row_of_tile, g, wg, wu, wd, fn, *, tm, tf, final_norm):
    t, d = x.shape
    f = wg.shape[1]
    return pl.pallas_call(
        functools.partial(_ffn_kernel, final_norm=final_norm),
        grid=(t // tm, f // tf),
        in_specs=[
            pl.BlockSpec((tm, d), lambda i, j: (i, 0)),
            pl.BlockSpec((1, d), lambda i, j: (0, 0)),
            _mod_spec(d, row_of_tile, mod_base),
            _mod_spec(d, row_of_tile, mod_base + 1),
            _mod_spec(d, row_of_tile, mod_base + 2),
            pl.BlockSpec((d, tf), lambda i, j: (0, j)),
            pl.BlockSpec((d, tf), lambda i, j: (0, j)),
            pl.BlockSpec((tf, d), lambda i, j: (j, 0)),
            pl.BlockSpec((1, d), lambda i, j: (0, 0)),
        ],
        out_specs=pl.BlockSpec((tm, d), lambda i, j: (i, 0)),
        out_shape=jax.ShapeDtypeStruct((t, d), F32),
        scratch_shapes=[pltpu.VMEM((tm, d), BF16)],
        compiler_params=_params(("parallel", "arbitrary"), 63),
        name="swiglu_half_step",
    )(x, g, mods, mods, mods, wg, wu, wd, fn)


def _rope_a(x, cos, sin):
    return x * cos + pltpu.roll(x, A_HEAD_DIM // 2, 1) * sin


def _rope_b(x, cos, sin_lo, sin_hi):
    half = B_ROPE // 2
    return x * cos + pltpu.roll(x, LANES - half, 1) * sin_lo + pltpu.roll(x, half, 1) * sin_hi


def _inproj_kernel(x_ref, g_ref, sh_ref, sc_ref, win_ref, wtail_ref, qn_ref, wuq_ref, kvn_ref, wuk_ref, wuvt_ref,
                   cosa_ref, sina_ref, cosb_ref, sinlo_ref, sinhi_ref,
                   qa_ref, ka_ref, va_ref, qb_ref, kb_ref, vbt_ref):
    rb = min(INPROJ_ROWS, x_ref.shape[0])
    for r0 in range(0, x_ref.shape[0], rb):
        rows = slice(r0, r0 + rb)
        h = _rms_mod(x_ref[rows, :], g_ref[...], sh_ref[0], sc_ref[0]).astype(BF16)
        cosa, sina = cosa_ref[rows, :], sina_ref[rows, :]
        cosb, sinlo, sinhi = cosb_ref[rows, :], sinlo_ref[rows, :], sinhi_ref[rows, :]

        aq = _dot(h, win_ref[:, 0:OFF_AK]) * (A_HEAD_DIM ** -0.5 * LOG2_E)
        for s in range(A_HEADS):
            sl = slice(s * LANES, (s + 1) * LANES)
            qa_ref[rows, sl] = _rope_a(aq[:, sl], cosa, sina).astype(BF16)
        akv = _dot(h, win_ref[:, OFF_AK:OFF_CQ])
        for s in range(A_KV_HEADS):
            sl = slice(s * LANES, (s + 1) * LANES)
            ka_ref[rows, sl] = _rope_a(akv[:, sl], cosa, sina).astype(BF16)
        va_ref[rows, :] = akv[:, A_KV_W:].astype(BF16)

        cq = _dot(h, win_ref[:, OFF_CQ:OFF_CKV])
        cqn = (cq * lax.rsqrt(jnp.mean(cq * cq, axis=-1, keepdims=True) + EPS) * qn_ref[...]).astype(BF16)
        qb = _dot(cqn, wuq_ref[...]) * (B_QK ** -0.5 * LOG2_E)
        for hd in range(B_HEADS):
            lo = hd * B_PAD
            qb_ref[rows, lo:lo + LANES] = qb[:, lo:lo + LANES].astype(BF16)
            qb_ref[rows, lo + LANES:lo + B_PAD] = _rope_b(qb[:, lo + LANES:lo + B_PAD], cosb, sinlo,
                                                          sinhi).astype(BF16)

        ckr = _dot(h, wtail_ref[...])
        ckv = ckr[:, 0:B_KV_RANK]
        ckvn = (ckv * lax.rsqrt(jnp.mean(ckv * ckv, axis=-1, keepdims=True) + EPS) * kvn_ref[...]).astype(BF16)
        kr = _rope_b(ckr[:, B_KV_RANK:], cosb, sinlo, sinhi).astype(BF16)
        kn = _dot(ckvn, wuk_ref[...])
        for hd in range(B_HEADS):
            lo = hd * B_PAD
            kb_ref[rows, lo:lo + LANES] = kn[:, hd * B_NOPE:(hd + 1) * B_NOPE].astype(BF16)
            kb_ref[rows, lo + LANES:lo + B_PAD] = kr
        vt = _dot_nt(wuvt_ref[...], ckvn).astype(BF16)
        ones_row = (lax.broadcasted_iota(jnp.int32, (B_VA - B_V, rb), 0) == 0).astype(F32).astype(BF16)
        for hd in range(B_HEADS):
            vbt_ref[0, hd * B_VA:hd * B_VA + B_V, rows] = vt[hd * B_V:(hd + 1) * B_V]
            vbt_ref[0, hd * B_VA + B_V:(hd + 1) * B_VA, rows] = ones_row


def _inproj(x, mods, row_of_tile, g, weights, tables, table_tile, *, tm):
    t, d = x.shape
    tok = lambda w: pl.BlockSpec((tm, w), lambda i: (i, 0))
    tab = pl.BlockSpec((tm, LANES), lambda i: (table_tile(i), 0))
    widths = (A_Q_W, A_KV_W, A_KV_W, B_HEADS * B_PAD, B_HEADS * B_PAD)
    vt_rows = B_HEADS * B_VA
    return pl.pallas_call(
        _inproj_kernel,
        grid=(t // tm,),
        in_specs=[
            tok(d),
            _const_spec((1, d)),
            _mod_spec(d, row_of_tile, 3),
            _mod_spec(d, row_of_tile, 4),
            *[_const_spec(w.shape) for w in weights],
            tab, tab, tab, tab, tab,
        ],
        out_specs=[tok(w) for w in widths] + [pl.BlockSpec((1, vt_rows, tm), lambda i: (i, 0, 0))],
        out_shape=[jax.ShapeDtypeStruct((t, w), BF16) for w in widths]
        + [jax.ShapeDtypeStruct((t // tm, vt_rows, tm), BF16)],
        compiler_params=_params(("parallel",), 56),
        name="attn_in_proj",
    )(x, g, mods, mods, *weights, *tables)


def _window_kernel(sink_ref, q_ref, k_ref, v_ref, kc_ref, vc_ref, o_ref, qt_ref, *, tq, n):
    qi = pl.program_id(1)
    span = tq + 2 * A_WINDOW
    start = pl.multiple_of(jnp.clip(qi * tq - A_WINDOW, 0, n - span), A_WINDOW)
    kpos = start + lax.broadcasted_iota(jnp.int32, (span, tq), 0)
    qpos = qi * tq + lax.broadcasted_iota(jnp.int32, (span, tq), 1)
    nc = kc_ref.shape[1]
    bias = jnp.concatenate([jnp.where(jnp.abs(kpos - qpos) <= A_WINDOW, 0.0, NEG_BIG), jnp.zeros((nc, tq), F32)],
                           axis=0)
    for g in range(A_KV_HEADS):
        gl = slice(g * A_HEAD_DIM, (g + 1) * A_HEAD_DIM)
        heads = [slice((g * A_REP + r) * A_HEAD_DIM, (g * A_REP + r + 1) * A_HEAD_DIM) for r in range(A_REP)]
        k = jnp.concatenate([k_ref[0, pl.ds(start, span), gl], kc_ref[0, :, gl]], axis=0)
        v = jnp.concatenate([v_ref[0, pl.ds(start, span), gl], vc_ref[0, :, gl]], axis=0)
        for r, hl in enumerate(heads):
            qt_ref[:, r * tq:(r + 1) * tq] = q_ref[0, :, hl].T
        s = _dot(k, qt_ref[...])
        s = jnp.concatenate([s[:, r * tq:(r + 1) * tq] + bias for r in range(A_REP)], axis=1)
        sink = jnp.concatenate([jnp.full((1, tq), sink_ref[g * A_REP + r] * LOG2_E, F32) for r in range(A_REP)],
                               axis=1)
        m = jnp.maximum(jnp.max(s, axis=0, keepdims=True), sink)
        p = jnp.exp2(s - m)
        denom = jnp.sum(p, axis=0, keepdims=True) + jnp.exp2(sink - m)
        o = jnp.transpose(_dot_tn(v, p.astype(BF16)) / denom)
        for r, hl in enumerate(heads):
            o_ref[0, :, hl] = o[r * tq:(r + 1) * tq].astype(BF16)


def _window_gqa(qa, ka, va, kca, vca, sink, *, tq):
    b, n, _ = qa.shape
    nc = kca.shape[1]
    assert n >= tq + 2 * A_WINDOW
    return pl.pallas_call(
        functools.partial(_window_kernel, tq=tq, n=n),
        grid=(b, n // tq),
        in_specs=[
            pl.BlockSpec(memory_space=pltpu.SMEM),
            pl.BlockSpec((1, tq, A_Q_W), lambda bi, qi: (bi, qi, 0)),
            pl.BlockSpec((1, n, A_KV_W), lambda bi, qi: (bi, 0, 0)),
            pl.BlockSpec((1, n, A_KV_W), lambda bi, qi: (bi, 0, 0)),
            pl.BlockSpec((1, nc, A_KV_W), lambda bi, qi: (bi, 0, 0)),
            pl.BlockSpec((1, nc, A_KV_W), lambda bi, qi: (bi, 0, 0)),
        ],
        out_specs=pl.BlockSpec((1, tq, A_Q_W), lambda bi, qi: (bi, qi, 0)),
        out_shape=jax.ShapeDtypeStruct((b, n, A_Q_W), BF16),
        scratch_shapes=[pltpu.VMEM((A_HEAD_DIM, A_REP * tq), BF16)],
        compiler_params=_params(("parallel", "arbitrary"), 48),
        name="window_gqa",
    )(sink, qa, ka, va, kca, vca)


def _mla_kernel(q_ref, k_ref, vt_ref, kc_ref, vct_ref, *rest, tk, n_cast):
    cast_src, o_ref, cast_dst = rest[:n_cast], rest[n_cast], rest[n_cast + 1:2 * n_cast + 1]
    sa_ref, sb_ref, qt_ref = rest[2 * n_cast + 1:]
    for src, dst in zip(cast_src, cast_dst):
        dst[...] = src[...].astype(BF16)

    tq = qt_ref.shape[2]
    for sub in range(qt_ref.shape[0]):
        rows = slice(sub * tq, (sub + 1) * tq)
        _mla_subtile(q_ref.at[0, rows, :], k_ref, vt_ref, kc_ref, vct_ref, o_ref.at[0, rows, :],
                     sa_ref.at[sub], sb_ref.at[sub], qt_ref.at[sub], tk=tk)


def _mla_subtile(q_ref, k_ref, vt_ref, kc_ref, vct_ref, o_ref, sa_ref, sb_ref, qt_ref, *, tk):
    qt_ref[...] = q_ref[...].T
    qt = qt_ref[...]
    nchunks = vt_ref.shape[1]

    def stage_scores(buf, c):
        s = _dot(k_ref[0, pl.ds(pl.multiple_of(c * tk, tk), tk), :], qt)
        buf[...] = s
        return jnp.max(s, axis=0, keepdims=True)

    def probs(s, m):
        return jnp.exp2(s - m).astype(BF16)

    def update(s, s_max, vt, m_old, acc_old):
        m = jnp.maximum(m_old, s_max)
        acc = jnp.exp2(m_old - m) * acc_old + _dot(vt, probs(s, m))
        return m, acc

    bufs = (sa_ref, sb_ref)
    next_max = stage_scores(bufs[0], 0)
    s = _dot(kc_ref[0], qt)
    m = jnp.max(s, axis=0, keepdims=True)
    carry = (m, _dot(vct_ref[0, 0], probs(s, m)), next_max)

    def group(c0, carry, last):
        m, acc, cur_max = carry
        for j in range(MLA_GROUP):
            next_max = cur_max
            if not (last and j == MLA_GROUP - 1):
                next_max = stage_scores(bufs[(j + 1) % 2], c0 + j + 1)
            m, acc = update(bufs[j % 2][...], cur_max, vt_ref[0, c0 + j], m, acc)
            cur_max = next_max
        return m, acc, cur_max

    carry = lax.fori_loop(0, nchunks // MLA_GROUP - 1,
                          lambda i, cr: group(i * MLA_GROUP, cr, False), carry)
    _, acc, _ = group(nchunks - MLA_GROUP, carry, True)
    o_ref[...] = jnp.transpose(acc[0:B_V] / acc[B_V:B_V + 1]).astype(BF16)


def _cast_split(rows, cols, steps):
    for cb in range(1, steps + 1):
        rb = steps // cb
        if (steps % cb == 0 and rows % rb == 0 and cols % cb == 0
                and (rows // rb) % BF16_SUBLANES == 0 and (cols // cb) % LANES == 0):
            return rb, cb
    raise ValueError(f"cannot split a ({rows}, {cols}) matrix into {steps} aligned blocks")


def _mla_attention(qb, kb, vbt, kcb, vcbt, casts, *, tq, n_sub):
    b, n, _ = qb.shape
    nc = kcb.shape[1]
    nchunks, _, tk = vbt.shape[1:]
    tq_blk = tq * n_sub
    nq = n // tq_blk
    assert nchunks * tk == n and nchunks % MLA_GROUP == 0 and vcbt.shape[1] == 1 and vcbt.shape[3] == nc
    steps = b * B_HEADS * nq

    cast_in, cast_out, cast_shapes = [], [], []
    for w, layer in casts:
        _, rows, cols = w.shape
        rb, cb = _cast_split(rows, cols, steps)
        blk = (rows // rb, cols // cb)
        step = lambda bi, hi, qi: (bi * B_HEADS + hi) * nq + qi
        cast_in.append(pl.BlockSpec((None,) + blk,
                                    lambda bi, hi, qi, layer=layer, cb=cb: (layer, step(bi, hi, qi) // cb,
                                                                            step(bi, hi, qi) % cb)))
        cast_out.append(pl.BlockSpec(blk, lambda bi, hi, qi, cb=cb: (step(bi, hi, qi) // cb, step(bi, hi, qi) % cb)))
        cast_shapes.append(jax.ShapeDtypeStruct((rows, cols), BF16))

    out = pl.pallas_call(
        functools.partial(_mla_kernel, tk=tk, n_cast=len(casts)),
        grid=(b, B_HEADS, nq),
        in_specs=[
            pl.BlockSpec((1, tq_blk, B_PAD), lambda bi, hi, qi: (bi, qi, hi)),
            pl.BlockSpec((1, n, B_PAD), lambda bi, hi, qi: (bi, 0, hi)),
            pl.BlockSpec((1, nchunks, B_VA, tk), lambda bi, hi, qi: (bi, 0, hi, 0)),
            pl.BlockSpec((1, nc, B_PAD), lambda bi, hi, qi: (bi, 0, hi)),
            pl.BlockSpec((1, 1, B_VA, nc), lambda bi, hi, qi: (bi, 0, hi, 0)),
        ] + cast_in,
        out_specs=[pl.BlockSpec((1, tq_blk, B_V), lambda bi, hi, qi: (bi, qi, hi))] + cast_out,
        out_shape=[jax.ShapeDtypeStruct((b, n, B_HEADS * B_V), BF16)] + cast_shapes,
        scratch_shapes=[pltpu.VMEM((n_sub, tk, tq), F32), pltpu.VMEM((n_sub, tk, tq), F32),
                        pltpu.VMEM((n_sub, B_PAD, tq), BF16)],
        compiler_params=_params(("arbitrary", "arbitrary", "arbitrary"), 56),
        name="mla_attention",
    )(qb, kb, vbt, kcb, vcbt, *[w for w, _ in casts])
    return out[0], out[1:]


def _outproj_kernel(x_ref, oa_ref, ob_ref, w_ref, gt_ref, o_ref):
    ka = oa_ref.shape[1]
    y = _dot(oa_ref[...], w_ref[0:ka, :]) + _dot(ob_ref[...], w_ref[ka:, :])
    o_ref[...] = x_ref[...] + gt_ref[0] * y


def _outproj(x, oa, ob, w, mods, row_of_tile, *, tm):
    t, d = x.shape
    return pl.pallas_call(
        _outproj_kernel,
        grid=(t // tm,),
        in_specs=[
            pl.BlockSpec((tm, d), lambda i: (i, 0)),
            pl.BlockSpec((tm, oa.shape[1]), lambda i: (i, 0)),
            pl.BlockSpec((tm, ob.shape[1]), lambda i: (i, 0)),
            _const_spec(w.shape),
            _mod_spec(d, row_of_tile, 5),
        ],
        out_specs=pl.BlockSpec((tm, d), lambda i: (i, 0)),
        out_shape=jax.ShapeDtypeStruct((t, d), F32),
        compiler_params=_params(("parallel",), 48),
        name="attn_out_proj",
    )(x, oa, ob, w, mods)


def _pool_kernel(x_ref, xp_ref, xn_ref, g_ref, sh_ref, sc_ref, gt_ref, w_ref, ps_ref, o_ref, hp_ref,
                 *, tm, n):
    i = pl.program_id(1)
    last = pl.num_programs(1) - 1
    g, sh, sc = g_ref[...], sh_ref[0], sc_ref[0]
    x = x_ref[0]
    hp_ref[0:POOL_HALO, :] = jnp.where(i > 0, _rms_mod(xp_ref[0], g, sh, sc), 0.0)
    hp_ref[POOL_HALO:POOL_HALO + tm, :] = _rms_mod(x, g, sh, sc)
    hp_ref[POOL_HALO + tm:, :] = jnp.where(i < last, _rms_mod(xn_ref[0], g, sh, sc), 0.0)

    t = i * tm + lax.broadcasted_iota(jnp.int32, (tm, 1), 0)
    gw = x.shape[1] // len(POOL_WINDOWS)
    for gi, w in enumerate(POOL_WINDOWS):
        cols = slice(gi * gw, (gi + 1) * gw)
        rows = tm + 2 * POOL_HALO
        shift_up = lambda a, k: a if k == 0 else pltpu.roll(a, rows - k, 0)
        tot = shift_up(hp_ref[:, cols], POOL_HALO - w // 2)
        k = 1
        while k < w:
            tot = tot + shift_up(tot, k)
            k *= 2
        tot = tot[0:tm]
        cnt =(jnp.minimum(t - w // 2 + w, n) - jnp.maximum(t - w // 2, 0)).astype(F32)
        diff = tot / cnt - hp_ref[pl.ds(POOL_HALO, tm), cols]
        y = _dot(diff.astype(BF16), w_ref[gi])
        o_ref[0, :, cols] = x[:, cols] + (gt_ref[0][:, cols] * ps_ref[:, cols]) * y


def _pool_mixer(x3, mods, g, w, ps, *, tm):
    b, n, d = x3.shape
    hb = tm // POOL_HALO
    mod = lambda k: pl.BlockSpec((1, 1, d), lambda bi, i: (bi * N_MOD + k, 0, 0))
    return pl.pallas_call(
        functools.partial(_pool_kernel, tm=tm, n=n),
        grid=(b, n // tm),
        in_specs=[
            pl.BlockSpec((1, tm, d), lambda bi, i: (bi, i, 0)),
            pl.BlockSpec((1, POOL_HALO, d), lambda bi, i: (bi, jnp.maximum(i * hb - 1, 0), 0)),
            pl.BlockSpec((1, POOL_HALO, d), lambda bi, i: (bi, jnp.minimum((i + 1) * hb, n // POOL_HALO - 1), 0)),
            pl.BlockSpec((1, d), lambda bi, i: (0, 0)),
            mod(3), mod(4), mod(5),
            pl.BlockSpec(w.shape, lambda bi, i: (0, 0, 0)),
            pl.BlockSpec((1, d), lambda bi, i: (0, 0)),
        ],
        out_specs=pl.BlockSpec((1, tm, d), lambda bi, i: (bi, i, 0)),
        out_shape=jax.ShapeDtypeStruct((b, n, d), F32),
        scratch_shapes=[pltpu.VMEM((tm + 2 * POOL_HALO, d), F32)],
        compiler_params=_params(("parallel", "arbitrary"), 48),
        name="pool_mixer",
    )(x3, x3, x3, g, mods, mods, mods, w, ps)


def _rope_tables(n):
    rows = n // GRID_W
    row = jnp.arange(rows, dtype=F32)
    col = jnp.arange(GRID_W, dtype=F32)

    def cos_sin(rot_dim):
        nf = rot_dim // 4
        inv = ROPE_BASE ** (-jnp.arange(nf, dtype=F32) / nf)
        expand = lambda fn: jnp.concatenate([jnp.repeat(fn(row[:, None] * inv), GRID_W, axis=0),
                                             jnp.tile(fn(col[:, None] * inv), (rows, 1))], axis=-1)
        return expand(jnp.cos), expand(jnp.sin)

    ca, sa = cos_sin(A_HEAD_DIM)
    cb, sb = cos_sin(B_ROPE)
    zb = jnp.zeros_like(cb)
    zpad = jnp.zeros((n, LANES - B_ROPE), F32)
    return (jnp.concatenate([ca, ca], axis=-1), jnp.concatenate([-sa, sa], axis=-1),
            jnp.concatenate([cb, cb, zpad], axis=-1), jnp.concatenate([-sb, zb, zpad], axis=-1),
            jnp.concatenate([zb, sb, zpad], axis=-1))


def _identity_tables(n):
    one = jnp.ones((n, LANES), F32)
    zero = jnp.zeros((n, LANES), F32)
    return (one, zero, one, zero, zero)


def _attn_weights(w_in, q_norm, w_uq, kv_norm, w_ukv):
    d = w_in.shape[0]
    head = w_in[:, :OFF_CKV].astype(BF16)
    tail = jnp.concatenate([w_in[:, OFF_CKV:], jnp.zeros((d, IN_W_PAD - IN_W), w_in.dtype)], axis=1).astype(BF16)
    uq = w_uq.reshape(B_Q_RANK, B_HEADS, B_QK)
    uq = jnp.concatenate([uq, jnp.zeros((B_Q_RANK, B_HEADS, B_PAD - B_QK), uq.dtype)], axis=-1)
    ukv = w_ukv.reshape(B_KV_RANK, B_HEADS, B_NOPE + B_V)
    uk = ukv[:, :, :B_NOPE].reshape(B_KV_RANK, B_HEADS * B_NOPE)
    uvt = ukv[:, :, B_NOPE:].reshape(B_KV_RANK, B_HEADS * B_V).T
    return (head, tail, q_norm.reshape(1, -1), uq.reshape(B_Q_RANK, B_HEADS * B_PAD).astype(BF16),
            kv_norm.reshape(1, -1), uk.astype(BF16), uvt.astype(BF16))


def kernel(x, c, ctx, c_ctx, w_ada, b_ada, norm_ffn1, norm_mix, norm_ffn2, ffn1_w_gate, ffn1_w_up, ffn1_w_down,
           ffn2_w_gate, ffn2_w_up, ffn2_w_down, attn_w_in, attn_sink, mla_q_norm, mla_w_uq, mla_kv_norm, mla_w_ukv,
           attn_w_out, pool_w, pool_scale, final_norm):
    b, n, d = x.shape
    nc = ctx.shape[1]
    depth = w_ada.shape[0]
    f = ffn1_w_gate.shape[2]
    t = b * n

    tm = min(512, n)
    tm_ffn = min(1024, n)
    tf = min(512, f)
    tmc = min(512, b * nc)
    tq_a = min(256, n)
    tq_b = min(1024, n)
    assert n % tm == 0 and f % tf == 0 and (b * nc) % tmc == 0 and n % GRID_W == 0

    tiles_per_sample = n // tm
    x_row = lambda i: i // tiles_per_sample
    ffn_row = lambda i: i // (n // tm_ffn)
    ctx_row = lambda i: b

    mods = _mods(c, c_ctx, w_ada, b_ada)
    rope = _rope_tables(n)
    fn = final_norm.reshape(1, d)

    ffn_f32 = ((ffn1_w_gate, ffn1_w_up, ffn1_w_down), (ffn2_w_gate, ffn2_w_up, ffn2_w_down))
    ffn_bf16 = {}

    def ffn_weights(family, layer):
        if (family, layer) not in ffn_bf16:
            ffn_bf16[(family, layer)] = tuple(w[layer].astype(BF16) for w in ffn_f32[family])
        return ffn_bf16[(family, layer)]

    xs = x.reshape(t, d)
    cx = ctx.reshape(b * nc, d)
    for l in range(depth):
        is_attn = l % 2 == 0
        ctx_out = any(j % 2 == 0 for j in range(l + 1, depth))
        ctx_in = is_attn or ctx_out
        i = l // 2
        m = mods[l]
        g1, gm, g2 = (v[l].reshape(1, d) for v in (norm_ffn1, norm_mix, norm_ffn2))
        w1 = ffn_weights(0, l)

        xs = _ffn(xs, m, 0, ffn_row, g1, *w1, fn, tm=tm_ffn, tf=tf, final_norm=False)
        if ctx_in:
            cx = _ffn(cx, m, 0, ctx_row, g1, *w1, fn, tm=tmc, tf=tf, final_norm=False)

        if is_attn:
            aw = _attn_weights(attn_w_in[i], mla_q_norm[i], mla_w_uq[i], mla_kv_norm[i], mla_w_ukv[i])
            qa, ka, va, qb, kb, vbt = _inproj(xs, m, x_row, gm, aw, rope, lambda ti: ti % tiles_per_sample, tm=tm)
            _, kca, vca, _, kcb, vcbt = _inproj(cx, m, ctx_row, gm, aw, _identity_tables(nc), lambda ti: 0, tm=nc)
            r3 = lambda a, rows: a.reshape(b, rows, a.shape[-1])
            r4 = lambda a: a.reshape(b, a.shape[0] // b, a.shape[1], a.shape[2])
            oa = _window_gqa(r3(qa, n), r3(ka, n), r3(va, n), r3(kca, nc), r3(vca, nc), attn_sink[i], tq=tq_a)
            pending = [(fam, lj) for lj in range(l, depth) for fam in (0, 1) if (lj, fam) > (l, 0)]
            ob, cast = _mla_attention(r3(qb, n), r3(kb, n), r4(vbt), r3(kcb, nc), r4(vcbt),
                                      [(w, lj) for fam, lj in pending for w in ffn_f32[fam]] + [(attn_w_out, i)],
                                      tq=tq_b, n_sub=min(MLA_SUBTILES, n // tq_b))
            for k, key in enumerate(pending):
                ffn_bf16[key] = tuple(cast[3 * k:3 * k + 3])
            xs = _outproj(xs, oa.reshape(t, -1), ob.reshape(t, -1), cast[-1], m, x_row, tm=tm)
            if ctx_out:
                raise NotImplementedError("context-stream attention output is not needed at this depth")
        else:
            xs = _pool_mixer(xs.reshape(b, n, d), m, gm, pool_w[i].astype(BF16), pool_scale[i].reshape(1, d),
                             tm=tm).reshape(t, d)
            if ctx_out:
                raise NotImplementedError("context-stream pooling output is not needed at this depth")

        w2 = ffn_weights(1, l)
        xs = _ffn(xs, m, 6, ffn_row, g2, *w2, fn, tm=tm_ffn, tf=tf, final_norm=(l == depth - 1))
        if ctx_out:
            cx = _ffn(cx, m, 6, ctx_row, g2, *w2, fn, tm=tmc, tf=tf, final_norm=False)
    return xs.reshape(b, n, d)
```

```python
import functools

import jax
import jax.numpy as jnp
from jax import lax
from jax.experimental import pallas as pl
from jax.experimental.pallas import tpu as pltpu

F32 = jnp.float32
BF16 = jnp.bfloat16

LANES = 128
SUBLANES = 8
BF16_SUBLANES = 16
MIB = 1 << 20

EPS = 1e-6
ROPE_BASE = 10000.0
GRID_W = 64
N_MOD = 9
A_HEADS = 8
A_KV_HEADS = 2
A_REP = A_HEADS // A_KV_HEADS
A_HEAD_DIM = 128
A_WINDOW = 128
B_HEADS = 8
B_NOPE = 128
B_ROPE = 64
B_QK = B_NOPE + B_ROPE
B_V = 128
B_Q_RANK = 768
B_KV_RANK = 256
B_PAD = 2 * LANES
B_VA = B_V + BF16_SUBLANES
A_Q_W = A_HEADS * A_HEAD_DIM
A_KV_W = A_KV_HEADS * A_HEAD_DIM
OFF_AK = A_Q_W
OFF_AV = OFF_AK + A_KV_W
OFF_CQ = OFF_AV + A_KV_W
OFF_CKV = OFF_CQ + B_Q_RANK
OFF_KR = OFF_CKV + B_KV_RANK
IN_W = OFF_KR + B_ROPE
IN_W_PAD = OFF_KR + LANES
POOL_WINDOWS = (2, 4, 8, 16)
POOL_HALO = SUBLANES
NEG_BIG = -1e30
LOG2_E = 1.4426950408889634
FFN_FIRST_STEP_ROWS = 512
INPROJ_ROWS = 512
MLA_SUBTILES = 2
WINDOW_SUBTILES = 2
MLA_GROUP = 4

MOD_ROWS = 8


def _params(semantics, vmem_mib):
    return pltpu.CompilerParams(dimension_semantics=semantics, vmem_limit_bytes=vmem_mib * MIB)


def _const_spec(shape):
    nd = len(shape)
    return pl.BlockSpec(shape, lambda *_: (0,) * nd, pipeline_mode=pl.Buffered(1))


def _rms_mod(x, g, shift, scale):
    y = x * lax.rsqrt(jnp.mean(x * x, axis=-1, keepdims=True) + EPS)
    return y * (g * (1.0 + scale)) + shift


def _dot(a, b):
    return jnp.dot(a, b, preferred_element_type=F32)


def _dot_nt(a, b):
    return lax.dot_general(a, b, (((1,), (1,)), ((), ())), preferred_element_type=F32)


def _dot_tn(a, b):
    return lax.dot_general(a, b, (((0,), (0,)), ((), ())), preferred_element_type=F32)


def _mods_kernel(c_ref, w_ref, b_ref, o_ref):
    c = c_ref[...]
    a = (c * jax.nn.sigmoid(c)).astype(BF16)
    o_ref[0] = _dot(a, w_ref[0].astype(BF16)) + b_ref[0]


def _mods(c, c_ctx, w_ada, b_ada):
    depth, d, nd = w_ada.shape
    b = c.shape[0]
    assert b + 1 <= MOD_ROWS
    cp = jnp.concatenate([c, c_ctx[None, :], jnp.zeros((MOD_ROWS - b - 1, d), F32)], axis=0)
    tn = min(d, 1024)
    assert nd % tn == 0
    out = pl.pallas_call(
        _mods_kernel,
        grid=(depth, nd // tn),
        in_specs=[
            pl.BlockSpec((MOD_ROWS, d), lambda l, j: (0, 0)),
            pl.BlockSpec((1, d, tn), lambda l, j: (l, 0, j)),
            pl.BlockSpec((1, 1, tn), lambda l, j: (l, 0, j)),
        ],
        out_specs=pl.BlockSpec((1, MOD_ROWS, tn), lambda l, j: (l, 0, j)),
        out_shape=jax.ShapeDtypeStruct((depth, MOD_ROWS, nd), F32),
        compiler_params=_params(("arbitrary", "arbitrary"), 40),
        name="adaln_mods",
    )(cp, w_ada, b_ada.reshape(depth, 1, nd))
    return out.reshape(depth, MOD_ROWS * N_MOD, 1, d)


def _mod_spec(d, row_of_tile, k):
    return pl.BlockSpec((1, 1, d), lambda i, *_: (row_of_tile(i) * N_MOD + k, 0, 0))


def _ffn_kernel(x_ref, g_ref, sh_ref, sc_ref, gt_ref, wg_ref, wu_ref, wd_ref, fn_ref,
                o_ref, h_ref, *, final_norm):
    j = pl.program_id(1)

    def accumulate(h, base):
        gate = _dot(h, wg_ref[...])
        up = _dot(h, wu_ref[...])
        a = (gate * jax.nn.sigmoid(gate)) * up
        return base + (0.5 * gt_ref[0]) * _dot(a.astype(BF16), wd_ref[...])

    @pl.when(j == 0)
    def _():
        gain = g_ref[...] * (1.0 + sc_ref[0])
        shift = sh_ref[0]
        rb = min(FFN_FIRST_STEP_ROWS, x_ref.shape[0])
        for r0 in range(0, x_ref.shape[0], rb):
            rows = slice(r0, r0 + rb)
            x = x_ref[rows, :]
            h = ((x * lax.rsqrt(jnp.mean(x * x, axis=-1, keepdims=True) + EPS)) * gain + shift).astype(BF16)
            h_ref[rows, :] = h
            o_ref[rows, :] = accumulate(h, x_ref[rows, :])

    last = pl.num_programs(1) - 1
    rms = lambda v: v * lax.rsqrt(jnp.mean(v * v, axis=-1, keepdims=True) + EPS) * fn_ref[...]

    @pl.when((j > 0) & ((j < last) | (not final_norm)))
    def _():
        o_ref[...] = accumulate(h_ref[...], o_ref[...])

    if final_norm:
        @pl.when((j > 0) & (j == last))
        def _():
            rb = min(FFN_FIRST_STEP_ROWS, x_ref.shape[0])
            for r0 in range(0, x_ref.shape[0], rb):
                rows = slice(r0, r0 + rb)
                o_ref[rows, :] = rms(accumulate(h_ref[rows, :], o_ref[rows, :]))

        @pl.when((j == 0) & (j == last))
        def _():
            o_ref[...] = rms(o_ref[...])


def _ffn(x, mods, mod_base, row_of_tile, g, wg, wu, wd, fn, *, tm, tf, final_norm):
    t, d = x.shape
    f = wg.shape[1]
    return pl.pallas_call(
        functools.partial(_ffn_kernel, final_norm=final_norm),
        grid=(t // tm, f // tf),
        in_specs=[
            pl.BlockSpec((tm, d), lambda i, j: (i, 0)),
            pl.BlockSpec((1, d), lambda i, j: (0, 0)),
            _mod_spec(d, row_of_tile, mod_base),
            _mod_spec(d, row_of_tile, mod_base + 1),
            _mod_spec(d, row_of_tile, mod_base + 2),
            pl.BlockSpec((d, tf), lambda i, j: (0, j)),
            pl.BlockSpec((d, tf), lambda i, j: (0, j)),
            pl.BlockSpec((tf, d), lambda i, j: (j, 0)),
            pl.BlockSpec((1, d), lambda i, j: (0, 0)),
        ],
        out_specs=pl.BlockSpec((tm, d), lambda i, j: (i, 0)),
        out_shape=jax.ShapeDtypeStruct((t, d), F32),
        scratch_shapes=[pltpu.VMEM((tm, d), BF16)],
        compiler_params=_params(("parallel", "arbitrary"), 63),
        name="swiglu_half_step",
    )(x, g, mods, mods, mods, wg, wu, wd, fn)


def _rope_a(x, cos, sin):
    return x * cos + pltpu.roll(x, A_HEAD_DIM // 2, 1) * sin


def _rope_b(x, cos, sin_lo, sin_hi):
    half = B_ROPE // 2
    return x * cos + pltpu.roll(x, LANES - half, 1) * sin_lo + pltpu.roll(x, half, 1) * sin_hi


def _inproj_kernel(x_ref, g_ref, sh_ref, sc_ref, win_ref, wtail_ref, qn_ref, wuq_ref, kvn_ref, wuk_ref, wuvt_ref,
                   cosa_ref, sina_ref, cosb_ref, sinlo_ref, sinhi_ref,
                   qa_ref, ka_ref, va_ref, qb_ref, kb_ref, vbt_ref):
    rb = min(INPROJ_ROWS, x_ref.shape[0])
    for r0 in range(0, x_ref.shape[0], rb):
        rows = slice(r0, r0 + rb)
        h = _rms_mod(x_ref[rows, :], g_ref[...], sh_ref[0], sc_ref[0]).astype(BF16)
        cosa, sina = cosa_ref[rows, :], sina_ref[rows, :]
        cosb, sinlo, sinhi = cosb_ref[rows, :], sinlo_ref[rows, :], sinhi_ref[rows, :]

        aq = _dot(h, win_ref[:, 0:OFF_AK]) * (A_HEAD_DIM ** -0.5 * LOG2_E)
        for s in range(A_HEADS):
            sl = slice(s * LANES, (s + 1) * LANES)
            qa_ref[rows, sl] = _rope_a(aq[:, sl], cosa, sina).astype(BF16)
        akv = _dot(h, win_ref[:, OFF_AK:OFF_CQ])
        for s in range(A_KV_HEADS):
            sl = slice(s * LANES, (s + 1) * LANES)
            ka_ref[rows, sl] = _rope_a(akv[:, sl], cosa, sina).astype(BF16)
        va_ref[rows, :] = akv[:, A_KV_W:].astype(BF16)

        cq = _dot(h, win_ref[:, OFF_CQ:OFF_CKV])
        cqn = (cq * lax.rsqrt(jnp.mean(cq * cq, axis=-1, keepdims=True) + EPS) * qn_ref[...]).astype(BF16)
        qb = _dot(cqn, wuq_ref[...]) * (B_QK ** -0.5 * LOG2_E)
        for hd in range(B_HEADS):
            lo = hd * B_PAD
            qb_ref[rows, lo:lo + LANES] = qb[:, lo:lo + LANES].astype(BF16)
            qb_ref[rows, lo + LANES:lo + B_PAD] = _rope_b(qb[:, lo + LANES:lo + B_PAD], cosb, sinlo,
                                                          sinhi).astype(BF16)

        ckr = _dot(h, wtail_ref[...])
        ckv = ckr[:, 0:B_KV_RANK]
        ckvn = (ckv * lax.rsqrt(jnp.mean(ckv * ckv, axis=-1, keepdims=True) + EPS) * kvn_ref[...]).astype(BF16)
        kr = _rope_b(ckr[:, B_KV_RANK:], cosb, sinlo, sinhi).astype(BF16)
        kn = _dot(ckvn, wuk_ref[...])
        for hd in range(B_HEADS):
            lo = hd * B_PAD
            kb_ref[rows, lo:lo + LANES] = kn[:, hd * B_NOPE:(hd + 1) * B_NOPE].astype(BF16)
            kb_ref[rows, lo + LANES:lo + B_PAD] = kr
        vt = _dot_nt(wuvt_ref[...], ckvn).astype(BF16)
        ones_row = (lax.broadcasted_iota(jnp.int32, (B_VA - B_V, rb), 0) == 0).astype(F32).astype(BF16)
        for hd in range(B_HEADS):
            vbt_ref[0, hd * B_VA:hd * B_VA + B_V, rows] = vt[hd * B_V:(hd + 1) * B_V]
            vbt_ref[0, hd * B_VA + B_V:(hd + 1) * B_VA, rows] = ones_row


def _inproj(x, mods, row_of_tile, g, weights, tables, table_tile, *, tm):
    t, d = x.shape
    tok = lambda w: pl.BlockSpec((tm, w), lambda i: (i, 0))
    tab = pl.BlockSpec((tm, LANES), lambda i: (table_tile(i), 0))
    widths = (A_Q_W, A_KV_W, A_KV_W, B_HEADS * B_PAD, B_HEADS * B_PAD)
    vt_rows = B_HEADS * B_VA
    return pl.pallas_call(
        _inproj_kernel,
        grid=(t // tm,),
        in_specs=[
            tok(d),
            _const_spec((1, d)),
            _mod_spec(d, row_of_tile, 3),
            _mod_spec(d, row_of_tile, 4),
            *[_const_spec(w.shape) for w in weights],
            tab, tab, tab, tab, tab,
        ],
        out_specs=[tok(w) for w in widths] + [pl.BlockSpec((1, vt_rows, tm), lambda i: (i, 0, 0))],
        out_shape=[jax.ShapeDtypeStruct((t, w), BF16) for w in widths]
        + [jax.ShapeDtypeStruct((t // tm, vt_rows, tm), BF16)],
        compiler_params=_params(("parallel",), 56),
        name="attn_in_proj",
    )(x, g, mods, mods, *weights, *tables)


def _window_kernel(sink_ref, q_ref, k_ref, v_ref, kc_ref, vc_ref, o_ref, qt_ref, *, tq, n):
    n_sub = qt_ref.shape[0]
    for sub in range(n_sub):
        _window_subtile(sink_ref, q_ref.at[0, sub * tq:(sub + 1) * tq, :], k_ref, v_ref, kc_ref, vc_ref,
                        o_ref.at[0, sub * tq:(sub + 1) * tq, :], qt_ref.at[sub],
                        (pl.program_id(1) * n_sub + sub) * tq, tq=tq, n=n)


def _window_subtile(sink_ref, q_ref, k_ref, v_ref, kc_ref, vc_ref, o_ref, qt_ref, q0, *, tq, n):
    span = tq + 2 * A_WINDOW
    start = pl.multiple_of(jnp.clip(q0 - A_WINDOW, 0, n - span), A_WINDOW)
    kpos = start + lax.broadcasted_iota(jnp.int32, (span, tq), 0)
    qpos = q0 + lax.broadcasted_iota(jnp.int32, (span, tq), 1)
    nc = kc_ref.shape[1]
    bias = jnp.concatenate([jnp.where(jnp.abs(kpos - qpos) <= A_WINDOW, 0.0, NEG_BIG), jnp.zeros((nc, tq), F32)],
                           axis=0)
    for g in range(A_KV_HEADS):
        gl = slice(g * A_HEAD_DIM, (g + 1) * A_HEAD_DIM)
        heads = [slice((g * A_REP + r) * A_HEAD_DIM, (g * A_REP + r + 1) * A_HEAD_DIM) for r in range(A_REP)]
        k = jnp.concatenate([k_ref[0, pl.ds(start, span), gl], kc_ref[0, :, gl]], axis=0)
        v = jnp.concatenate([v_ref[0, pl.ds(start, span), gl], vc_ref[0, :, gl]], axis=0)
        for r, hl in enumerate(heads):
            qt_ref[:, r * tq:(r + 1) * tq] = q_ref[:, hl].T
        s = _dot(k, qt_ref[...])
        s = jnp.concatenate([s[:, r * tq:(r + 1) * tq] + bias for r in range(A_REP)], axis=1)
        sink = jnp.concatenate([jnp.full((1, tq), sink_ref[g * A_REP + r] * LOG2_E, F32) for r in range(A_REP)],
                               axis=1)
        m = jnp.maximum(jnp.max(s, axis=0, keepdims=True), sink)
        p = jnp.exp2(s - m)
        denom = jnp.sum(p, axis=0, keepdims=True) + jnp.exp2(sink - m)
        o = jnp.transpose(_dot_tn(v, p.astype(BF16)) / denom)
        for r, hl in enumerate(heads):
            o_ref[:, hl] = o[r * tq:(r + 1) * tq].astype(BF16)


def _window_gqa(qa, ka, va, kca, vca, sink, *, tq, n_sub):
    b, n, _ = qa.shape
    nc = kca.shape[1]
    assert n >= tq + 2 * A_WINDOW
    tq_blk = tq * n_sub
    return pl.pallas_call(
        functools.partial(_window_kernel, tq=tq, n=n),
        grid=(b, n // tq_blk),
        in_specs=[
            pl.BlockSpec(memory_space=pltpu.SMEM),
            pl.BlockSpec((1, tq_blk, A_Q_W), lambda bi, qi: (bi, qi, 0)),
            pl.BlockSpec((1, n, A_KV_W), lambda bi, qi: (bi, 0, 0)),
            pl.BlockSpec((1, n, A_KV_W), lambda bi, qi: (bi, 0, 0)),
            pl.BlockSpec((1, nc, A_KV_W), lambda bi, qi: (bi, 0, 0)),
            pl.BlockSpec((1, nc, A_KV_W), lambda bi, qi: (bi, 0, 0)),
        ],
        out_specs=pl.BlockSpec((1, tq_blk, A_Q_W), lambda bi, qi: (bi, qi, 0)),
        out_shape=jax.ShapeDtypeStruct((b, n, A_Q_W), BF16),
        scratch_shapes=[pltpu.VMEM((n_sub, A_HEAD_DIM, A_REP * tq), BF16)],
        compiler_params=_params(("parallel", "arbitrary"), 48),
        name="window_gqa",
    )(sink, qa, ka, va, kca, vca)


def _mla_kernel(q_ref, k_ref, vt_ref, kc_ref, vct_ref, *rest, tk, n_cast):
    cast_src, o_ref, cast_dst = rest[:n_cast], rest[n_cast], rest[n_cast + 1:2 * n_cast + 1]
    sa_ref, sb_ref, qt_ref = rest[2 * n_cast + 1:]
    for src, dst in zip(cast_src, cast_dst):
        dst[...] = src[...].astype(BF16)

    tq = qt_ref.shape[2]
    for sub in range(qt_ref.shape[0]):
        rows = slice(sub * tq, (sub + 1) * tq)
        _mla_subtile(q_ref.at[0, rows, :], k_ref, vt_ref, kc_ref, vct_ref, o_ref.at[0, rows, :],
                     sa_ref.at[sub], sb_ref.at[sub], qt_ref.at[sub], tk=tk)


def _mla_subtile(q_ref, k_ref, vt_ref, kc_ref, vct_ref, o_ref, sa_ref, sb_ref, qt_ref, *, tk):
    qt_ref[...] = q_ref[...].T
    qt = qt_ref[...]
    nchunks = vt_ref.shape[1]

    def stage_scores(buf, c):
        s = _dot(k_ref[0, pl.ds(pl.multiple_of(c * tk, tk), tk), :], qt)
        buf[...] = s
        return jnp.max(s, axis=0, keepdims=True)

    def probs(s, m):
        return jnp.exp2(s - m).astype(BF16)

    def update(s, s_max, vt, m_old, acc_old):
        m = jnp.maximum(m_old, s_max)
        acc = jnp.exp2(m_old - m) * acc_old + _dot(vt, probs(s, m))
        return m, acc

    bufs = (sa_ref, sb_ref)
    next_max = stage_scores(bufs[0], 0)
    s = _dot(kc_ref[0], qt)
    m = jnp.max(s, axis=0, keepdims=True)
    carry = (m, _dot(vct_ref[0, 0], probs(s, m)), next_max)

    def group(c0, carry, last):
        m, acc, cur_max = carry
        for j in range(MLA_GROUP):
            next_max = cur_max
            if not (last and j == MLA_GROUP - 1):
                next_max = stage_scores(bufs[(j + 1) % 2], c0 + j + 1)
            m, acc = update(bufs[j % 2][...], cur_max, vt_ref[0, c0 + j], m, acc)
            cur_max = next_max
        return m, acc, cur_max

    carry = lax.fori_loop(0, nchunks // MLA_GROUP - 1,
                          lambda i, cr: group(i * MLA_GROUP, cr, False), carry)
    _, acc, _ = group(nchunks - MLA_GROUP, carry, True)
    o_ref[...] = jnp.transpose(acc[0:B_V] / acc[B_V:B_V + 1]).astype(BF16)


def _cast_split(rows, cols, steps):
    for cb in range(1, steps + 1):
        rb = steps // cb
        if (steps % cb == 0 and rows % rb == 0 and cols % cb == 0
                and (rows // rb) % BF16_SUBLANES == 0 and (cols // cb) % LANES == 0):
            return rb, cb
    raise ValueError(f"cannot split a ({rows}, {cols}) matrix into {steps} aligned blocks")


def _mla_attention(qb, kb, vbt, kcb, vcbt, casts, *, tq, n_sub):
    b, n, _ = qb.shape
    nc = kcb.shape[1]
    nchunks, _, tk = vbt.shape[1:]
    tq_blk = tq * n_sub
    nq = n // tq_blk
    assert nchunks * tk == n and nchunks % MLA_GROUP == 0 and vcbt.shape[1] == 1 and vcbt.shape[3] == nc
    steps = b * B_HEADS * nq

    cast_in, cast_out, cast_shapes = [], [], []
    for w, layer in casts:
        _, rows, cols = w.shape
        rb, cb = _cast_split(rows, cols, steps)
        blk = (rows // rb, cols // cb)
        step = lambda bi, hi, qi: (bi * B_HEADS + hi) * nq + qi
        cast_in.append(pl.BlockSpec((None,) + blk,
                                    lambda bi, hi, qi, layer=layer, cb=cb: (layer, step(bi, hi, qi) // cb,
                                                                            step(bi, hi, qi) % cb)))
        cast_out.append(pl.BlockSpec(blk, lambda bi, hi, qi, cb=cb: (step(bi, hi, qi) // cb, step(bi, hi, qi) % cb)))
        cast_shapes.append(jax.ShapeDtypeStruct((rows, cols), BF16))

    out = pl.pallas_call(
        functools.partial(_mla_kernel, tk=tk, n_cast=len(casts)),
        grid=(b, B_HEADS, nq),
        in_specs=[
            pl.BlockSpec((1, tq_blk, B_PAD), lambda bi, hi, qi: (bi, qi, hi)),
            pl.BlockSpec((1, n, B_PAD), lambda bi, hi, qi: (bi, 0, hi)),
            pl.BlockSpec((1, nchunks, B_VA, tk), lambda bi, hi, qi: (bi, 0, hi, 0)),
            pl.BlockSpec((1, nc, B_PAD), lambda bi, hi, qi: (bi, 0, hi)),
            pl.BlockSpec((1, 1, B_VA, nc), lambda bi, hi, qi: (bi, 0, hi, 0)),
        ] + cast_in,
        out_specs=[pl.BlockSpec((1, tq_blk, B_V), lambda bi, hi, qi: (bi, qi, hi))] + cast_out,
        out_shape=[jax.ShapeDtypeStruct((b, n, B_HEADS * B_V), BF16)] + cast_shapes,
        scratch_shapes=[pltpu.VMEM((n_sub, tk, tq), F32), pltpu.VMEM((n_sub, tk, tq), F32),
                        pltpu.VMEM((n_sub, B_PAD, tq), BF16)],
        compiler_params=_params(("arbitrary", "arbitrary", "arbitrary"), 56),
        name="mla_attention",
    )(qb, kb, vbt, kcb, vcbt, *[w for w, _ in casts])
    return out[0], out[1:]


def _outproj_kernel(x_ref, oa_ref, ob_ref, w_ref, gt_ref, o_ref):
    ka = oa_ref.shape[1]
    y = _dot(oa_ref[...], w_ref[0:ka, :]) + _dot(ob_ref[...], w_ref[ka:, :])
    o_ref[...] = x_ref[...] + gt_ref[0] * y


def _outproj(x, oa, ob, w, mods, row_of_tile, *, tm):
    t, d = x.shape
    return pl.pallas_call(
        _outproj_kernel,
        grid=(t // tm,),
        in_specs=[
            pl.BlockSpec((tm, d), lambda i: (i, 0)),
            pl.BlockSpec((tm, oa.shape[1]), lambda i: (i, 0)),
            pl.BlockSpec((tm, ob.shape[1]), lambda i: (i, 0)),
            _const_spec(w.shape),
            _mod_spec(d, row_of_tile, 5),
        ],
        out_specs=pl.BlockSpec((tm, d), lambda i: (i, 0)),
        out_shape=jax.ShapeDtypeStruct((t, d), F32),
        compiler_params=_params(("parallel",), 48),
        name="attn_out_proj",
    )(x, oa, ob, w, mods)


def _pool_kernel(x_ref, xp_ref, xn_ref, g_ref, sh_ref, sc_ref, gt_ref, w_ref, ps_ref, o_ref, hp_ref,
                 *, tm, n):
    i = pl.program_id(1)
    last = pl.num_programs(1) - 1
    g, sh, sc = g_ref[...], sh_ref[0], sc_ref[0]
    x = x_ref[0]
    hp_ref[0:POOL_HALO, :] = jnp.where(i > 0, _rms_mod(xp_ref[0], g, sh, sc), 0.0)
    hp_ref[POOL_HALO:POOL_HALO + tm, :] = _rms_mod(x, g, sh, sc)
    hp_ref[POOL_HALO + tm:, :] = jnp.where(i < last, _rms_mod(xn_ref[0], g, sh, sc), 0.0)

    t = i * tm + lax.broadcasted_iota(jnp.int32, (tm, 1), 0)
    gw = x.shape[1] // len(POOL_WINDOWS)
    for gi, w in enumerate(POOL_WINDOWS):
        cols = slice(gi * gw, (gi + 1) * gw)
        rows = tm + 2 * POOL_HALO
        shift_up = lambda a, k: a if k == 0 else pltpu.roll(a, rows - k, 0)
        tot = shift_up(hp_ref[:, cols], POOL_HALO - w // 2)
        k = 1
        while k < w:
            tot = tot + shift_up(tot, k)
            k *= 2
        tot = tot[0:tm]
        cnt =(jnp.minimum(t - w // 2 + w, n) - jnp.maximum(t - w // 2, 0)).astype(F32)
        diff = tot / cnt - hp_ref[pl.ds(POOL_HALO, tm), cols]
        y = _dot(diff.astype(BF16), w_ref[gi])
        o_ref[0, :, cols] = x[:, cols] + (gt_ref[0][:, cols] * ps_ref[:, cols]) * y


def _pool_mixer(x3, mods, g, w, ps, *, tm):
    b, n, d = x3.shape
    hb = tm // POOL_HALO
    mod = lambda k: pl.BlockSpec((1, 1, d), lambda bi, i: (bi * N_MOD + k, 0, 0))
    return pl.pallas_call(
        functools.partial(_pool_kernel, tm=tm, n=n),
        grid=(b, n // tm),
        in_specs=[
            pl.BlockSpec((1, tm, d), lambda bi, i: (bi, i, 0)),
            pl.BlockSpec((1, POOL_HALO, d), lambda bi, i: (bi, jnp.maximum(i * hb - 1, 0), 0)),
            pl.BlockSpec((1, POOL_HALO, d), lambda bi, i: (bi, jnp.minimum((i + 1) * hb, n // POOL_HALO - 1), 0)),
            pl.BlockSpec((1, d), lambda bi, i: (0, 0)),
            mod(3), mod(4), mod(5),
            pl.BlockSpec(w.shape, lambda bi, i: (0, 0, 0)),
            pl.BlockSpec((1, d), lambda bi, i: (0, 0)),
        ],
        out_specs=pl.BlockSpec((1, tm, d), lambda bi, i: (bi, i, 0)),
        out_shape=jax.ShapeDtypeStruct((b, n, d), F32),
        scratch_shapes=[pltpu.VMEM((tm + 2 * POOL_HALO, d), F32)],
        compiler_params=_params(("parallel", "arbitrary"), 48),
        name="pool_mixer",
    )(x3, x3, x3, g, mods, mods, mods, w, ps)


def _rope_tables(n):
    rows = n // GRID_W
    row = jnp.arange(rows, dtype=F32)
    col = jnp.arange(GRID_W, dtype=F32)

    def cos_sin(rot_dim):
        nf = rot_dim // 4
        inv = ROPE_BASE ** (-jnp.arange(nf, dtype=F32) / nf)
        expand = lambda fn: jnp.concatenate([jnp.repeat(fn(row[:, None] * inv), GRID_W, axis=0),
                                             jnp.tile(fn(col[:, None] * inv), (rows, 1))], axis=-1)
        return expand(jnp.cos), expand(jnp.sin)

    ca, sa = cos_sin(A_HEAD_DIM)
    cb, sb = cos_sin(B_ROPE)
    zb = jnp.zeros_like(cb)
    zpad = jnp.zeros((n, LANES - B_ROPE), F32)
    return (jnp.concatenate([ca, ca], axis=-1), jnp.concatenate([-sa, sa], axis=-1),
            jnp.concatenate([cb, cb, zpad], axis=-1), jnp.concatenate([-sb, zb, zpad], axis=-1),
            jnp.concatenate([zb, sb, zpad], axis=-1))


def _identity_tables(n):
    one = jnp.ones((n, LANES), F32)
    zero = jnp.zeros((n, LANES), F32)
    return (one, zero, one, zero, zero)


def _attn_weights(w_in, q_norm, w_uq, kv_norm, w_ukv):
    d = w_in.shape[0]
    head = w_in[:, :OFF_CKV].astype(BF16)
    tail = jnp.concatenate([w_in[:, OFF_CKV:], jnp.zeros((d, IN_W_PAD - IN_W), w_in.dtype)], axis=1).astype(BF16)
    uq = w_uq.reshape(B_Q_RANK, B_HEADS, B_QK)
    uq = jnp.concatenate([uq, jnp.zeros((B_Q_RANK, B_HEADS, B_PAD - B_QK), uq.dtype)], axis=-1)
    ukv = w_ukv.reshape(B_KV_RANK, B_HEADS, B_NOPE + B_V)
    uk = ukv[:, :, :B_NOPE].reshape(B_KV_RANK, B_HEADS * B_NOPE)
    uvt = ukv[:, :, B_NOPE:].reshape(B_KV_RANK, B_HEADS * B_V).T
    return (head, tail, q_norm.reshape(1, -1), uq.reshape(B_Q_RANK, B_HEADS * B_PAD).astype(BF16),
            kv_norm.reshape(1, -1), uk.astype(BF16), uvt.astype(BF16))


def kernel(x, c, ctx, c_ctx, w_ada, b_ada, norm_ffn1, norm_mix, norm_ffn2, ffn1_w_gate, ffn1_w_up, ffn1_w_down,
           ffn2_w_gate, ffn2_w_up, ffn2_w_down, attn_w_in, attn_sink, mla_q_norm, mla_w_uq, mla_kv_norm, mla_w_ukv,
           attn_w_out, pool_w, pool_scale, final_norm):
    b, n, d = x.shape
    nc = ctx.shape[1]
    depth = w_ada.shape[0]
    f = ffn1_w_gate.shape[2]
    t = b * n

    tm = min(512, n)
    tm_ffn = min(1024, n)
    tf = min(512, f)
    tmc = min(512, b * nc)
    tq_a = min(256, n)
    tq_b = min(1024, n)
    assert n % tm == 0 and f % tf == 0 and (b * nc) % tmc == 0 and n % GRID_W == 0

    tiles_per_sample = n // tm
    x_row = lambda i: i // tiles_per_sample
    ffn_row = lambda i: i // (n // tm_ffn)
    ctx_row = lambda i: b

    mods = _mods(c, c_ctx, w_ada, b_ada)
    rope = _rope_tables(n)
    fn = final_norm.reshape(1, d)

    ffn_f32 = ((ffn1_w_gate, ffn1_w_up, ffn1_w_down), (ffn2_w_gate, ffn2_w_up, ffn2_w_down))
    ffn_bf16 = {}

    def ffn_weights(family, layer):
        if (family, layer) not in ffn_bf16:
            ffn_bf16[(family, layer)] = tuple(w[layer].astype(BF16) for w in ffn_f32[family])
        return ffn_bf16[(family, layer)]

    xs = x.reshape(t, d)
    cx = ctx.reshape(b * nc, d)
    for l in range(depth):
        is_attn = l % 2 == 0
        ctx_out = any(j % 2 == 0 for j in range(l + 1, depth))
        ctx_in = is_attn or ctx_out
        i = l // 2
        m = mods[l]
        g1, gm, g2 = (v[l].reshape(1, d) for v in (norm_ffn1, norm_mix, norm_ffn2))
        w1 = ffn_weights(0, l)

        xs = _ffn(xs, m, 0, ffn_row, g1, *w1, fn, tm=tm_ffn, tf=tf, final_norm=False)
        if ctx_in:
            cx = _ffn(cx, m, 0, ctx_row, g1, *w1, fn, tm=tmc, tf=tf, final_norm=False)

        if is_attn:
            aw = _attn_weights(attn_w_in[i], mla_q_norm[i], mla_w_uq[i], mla_kv_norm[i], mla_w_ukv[i])
            qa, ka, va, qb, kb, vbt = _inproj(xs, m, x_row, gm, aw, rope, lambda ti: ti % tiles_per_sample, tm=tm)
            _, kca, vca, _, kcb, vcbt = _inproj(cx, m, ctx_row, gm, aw, _identity_tables(nc), lambda ti: 0, tm=nc)
            r3 = lambda a, rows: a.reshape(b, rows, a.shape[-1])
            r4 = lambda a: a.reshape(b, a.shape[0] // b, a.shape[1], a.shape[2])
            oa = _window_gqa(r3(qa, n), r3(ka, n), r3(va, n), r3(kca, nc), r3(vca, nc), attn_sink[i], tq=tq_a,
                             n_sub=min(WINDOW_SUBTILES, n // tq_a))
            pending = [(fam, lj) for lj in range(l, depth) for fam in (0, 1) if (lj, fam) > (l, 0)]
            ob, cast = _mla_attention(r3(qb, n), r3(kb, n), r4(vbt), r3(kcb, nc), r4(vcbt),
                                      [(w, lj) for fam, lj in pending for w in ffn_f32[fam]] + [(attn_w_out, i)],
                                      tq=tq_b, n_sub=min(MLA_SUBTILES, n // tq_b))
            for k, key in enumerate(pending):
                ffn_bf16[key] = tuple(cast[3 * k:3 * k + 3])
            xs = _outproj(xs, oa.reshape(t, -1), ob.reshape(t, -1), cast[-1], m, x_row, tm=tm)
            if ctx_out:
                raise NotImplementedError("context-stream attention output is not needed at this depth")
        else:
            xs = _pool_mixer(xs.reshape(b, n, d), m, gm, pool_w[i].astype(BF16), pool_scale[i].reshape(1, d),
                             tm=tm).reshape(t, d)
            if ctx_out:
                raise NotImplementedError("context-stream pooling output is not needed at this depth")

        w2 = ffn_weights(1, l)
        xs = _ffn(xs, m, 6, ffn_row, g2, *w2, fn, tm=tm_ffn, tf=tf, final_norm=(l == depth - 1))
        if ctx_out:
            cx = _ffn(cx, m, 6, ctx_row, g2, *w2, fn, tm=tmc, tf=tf, final_norm=False)
    return xs.reshape(b, n, d)
```

```python
import functools

import jax
import jax.numpy as jnp
from jax import lax
from jax.experimental import pallas as pl
from jax.experimental.pallas import tpu as pltpu

F32 = jnp.float32
BF16 = jnp.bfloat16

LANES = 128
SUBLANES = 8
BF16_SUBLANES = 16
MIB = 1 << 20

EPS = 1e-6
ROPE_BASE = 10000.0
GRID_W = 64
N_MOD = 9
A_HEADS = 8
A_KV_HEADS = 2
A_REP = A_HEADS // A_KV_HEADS
A_HEAD_DIM = 128
A_WINDOW = 128
B_HEADS = 8
B_NOPE = 128
B_ROPE = 64
B_QK = B_NOPE + B_ROPE
B_V = 128
B_Q_RANK = 768
B_KV_RANK = 256
B_PAD = 2 * LANES
B_VA = B_V + BF16_SUBLANES
A_Q_W = A_HEADS * A_HEAD_DIM
A_KV_W = A_KV_HEADS * A_HEAD_DIM
OFF_AK = A_Q_W
OFF_AV = OFF_AK + A_KV_W
OFF_CQ = OFF_AV + A_KV_W
OFF_CKV = OFF_CQ + B_Q_RANK
OFF_KR = OFF_CKV + B_KV_RANK
IN_W = OFF_KR + B_ROPE
IN_W_PAD = OFF_KR + LANES
POOL_WINDOWS = (2, 4, 8, 16)
POOL_HALO = SUBLANES
NEG_BIG = -1e30
LOG2_E = 1.4426950408889634
FFN_FIRST_STEP_ROWS = 512
INPROJ_ROWS = 512
MLA_SUBTILES = 2
WINDOW_SUBTILES = 4
MLA_GROUP = 4

MOD_ROWS = 8


def _params(semantics, vmem_mib):
    return pltpu.CompilerParams(dimension_semantics=semantics, vmem_limit_bytes=vmem_mib * MIB)


def _const_spec(shape):
    nd = len(shape)
    return pl.BlockSpec(shape, lambda *_: (0,) * nd, pipeline_mode=pl.Buffered(1))


def _rms_mod(x, g, shift, scale):
    y = x * lax.rsqrt(jnp.mean(x * x, axis=-1, keepdims=True) + EPS)
    return y * (g * (1.0 + scale)) + shift


def _dot(a, b):
    return jnp.dot(a, b, preferred_element_type=F32)


def _dot_nt(a, b):
    return lax.dot_general(a, b, (((1,), (1,)), ((), ())), preferred_element_type=F32)


def _dot_tn(a, b):
    return lax.dot_general(a, b, (((0,), (0,)), ((), ())), preferred_element_type=F32)


def _mods_kernel(c_ref, w_ref, b_ref, o_ref):
    c = c_ref[...]
    a = (c * jax.nn.sigmoid(c)).astype(BF16)
    o_ref[0] = _dot(a, w_ref[0].astype(BF16)) + b_ref[0]


def _mods(c, c_ctx, w_ada, b_ada):
    depth, d, nd = w_ada.shape
    b = c.shape[0]
    assert b + 1 <= MOD_ROWS
    cp = jnp.concatenate([c, c_ctx[None, :], jnp.zeros((MOD_ROWS - b - 1, d), F32)], axis=0)
    tn = min(d, 1024)
    assert nd % tn == 0
    out = pl.pallas_call(
        _mods_kernel,
        grid=(depth, nd // tn),
        in_specs=[
            pl.BlockSpec((MOD_ROWS, d), lambda l, j: (0, 0)),
            pl.BlockSpec((1, d, tn), lambda l, j: (l, 0, j)),
            pl.BlockSpec((1, 1, tn), lambda l, j: (l, 0, j)),
        ],
        out_specs=pl.BlockSpec((1, MOD_ROWS, tn), lambda l, j: (l, 0, j)),
        out_shape=jax.ShapeDtypeStruct((depth, MOD_ROWS, nd), F32),
        compiler_params=_params(("arbitrary", "arbitrary"), 40),
        name="adaln_mods",
    )(cp, w_ada, b_ada.reshape(depth, 1, nd))
    return out.reshape(depth, MOD_ROWS * N_MOD, 1, d)


def _mod_spec(d, row_of_tile, k):
    return pl.BlockSpec((1, 1, d), lambda i, *_: (row_of_tile(i) * N_MOD + k, 0, 0))


def _ffn_kernel(x_ref, g_ref, sh_ref, sc_ref, gt_ref, wg_ref, wu_ref, wd_ref, fn_ref,
                o_ref, h_ref, *, final_norm):
    j = pl.program_id(1)

    def accumulate(h, base):
        gate = _dot(h, wg_ref[...])
        up = _dot(h, wu_ref[...])
        a = (gate * jax.nn.sigmoid(gate)) * up
        return base + (0.5 * gt_ref[0]) * _dot(a.astype(BF16), wd_ref[...])

    @pl.when(j == 0)
    def _():
        gain = g_ref[...] * (1.0 + sc_ref[0])
        shift = sh_ref[0]
        rb = min(FFN_FIRST_STEP_ROWS, x_ref.shape[0])
        for r0 in range(0, x_ref.shape[0], rb):
            rows = slice(r0, r0 + rb)
            x = x_ref[rows, :]
            h = ((x * lax.rsqrt(jnp.mean(x * x, axis=-1, keepdims=True) + EPS)) * gain + shift).astype(BF16)
            h_ref[rows, :] = h
            o_ref[rows, :] = accumulate(h, x_ref[rows, :])

    last = pl.num_programs(1) - 1
    rms = lambda v: v * lax.rsqrt(jnp.mean(v * v, axis=-1, keepdims=True) + EPS) * fn_ref[...]

    @pl.when((j > 0) & ((j < last) | (not final_norm)))
    def _():
        o_ref[...] = accumulate(h_ref[...], o_ref[...])

    if final_norm:
        @pl.when((j > 0) & (j == last))
        def _():
            rb = min(FFN_FIRST_STEP_ROWS, x_ref.shape[0])
            for r0 in range(0, x_ref.shape[0], rb):
                rows = slice(r0, r0 + rb)
                o_ref[rows, :] = rms(accumulate(h_ref[rows, :], o_ref[rows, :]))

        @pl.when((j == 0) & (j == last))
        def _():
            o_ref[...] = rms(o_ref[...])


def _ffn(x, mods, mod_base, row_of_tile, g, wg, wu, wd, fn, *, tm, tf, final_norm):
    t, d = x.shape
    f = wg.shape[1]
    return pl.pallas_call(
        functools.partial(_ffn_kernel, final_norm=final_norm),
        grid=(t // tm, f // tf),
        in_specs=[
            pl.BlockSpec((tm, d), lambda i, j: (i, 0)),
            pl.BlockSpec((1, d), lambda i, j: (0, 0)),
            _mod_spec(d, row_of_tile, mod_base),
            _mod_spec(d, row_of_tile, mod_base + 1),
            _mod_spec(d, row_of_tile, mod_base + 2),
            pl.BlockSpec((d, tf), lambda i, j: (0, j)),
            pl.BlockSpec((d, tf), lambda i, j: (0, j)),
            pl.BlockSpec((tf, d), lambda i, j: (j, 0)),
            pl.BlockSpec((1, d), lambda i, j: (0, 0)),
        ],
        out_specs=pl.BlockSpec((tm, d), lambda i, j: (i, 0)),
        out_shape=jax.ShapeDtypeStruct((t, d), F32),
        scratch_shapes=[pltpu.VMEM((tm, d), BF16)],
        compiler_params=_params(("parallel", "arbitrary"), 63),
        name="swiglu_half_step",
    )(x, g, mods, mods, mods, wg, wu, wd, fn)


def _rope_a(x, cos, sin):
    return x * cos + pltpu.roll(x, A_HEAD_DIM // 2, 1) * sin


def _rope_b(x, cos, sin_lo, sin_hi):
    half = B_ROPE // 2
    return x * cos + pltpu.roll(x, LANES - half, 1) * sin_lo + pltpu.roll(x, half, 1) * sin_hi


def _inproj_kernel(x_ref, g_ref, sh_ref, sc_ref, win_ref, wtail_ref, qn_ref, wuq_ref, kvn_ref, wuk_ref, wuvt_ref,
                   cosa_ref, sina_ref, cosb_ref, sinlo_ref, sinhi_ref,
                   qa_ref, ka_ref, va_ref, qb_ref, kb_ref, vbt_ref):
    rb = min(INPROJ_ROWS, x_ref.shape[0])
    for r0 in range(0, x_ref.shape[0], rb):
        rows = slice(r0, r0 + rb)
        h = _rms_mod(x_ref[rows, :], g_ref[...], sh_ref[0], sc_ref[0]).astype(BF16)
        cosa, sina = cosa_ref[rows, :], sina_ref[rows, :]
        cosb, sinlo, sinhi = cosb_ref[rows, :], sinlo_ref[rows, :], sinhi_ref[rows, :]

        aq = _dot(h, win_ref[:, 0:OFF_AK]) * (A_HEAD_DIM ** -0.5 * LOG2_E)
        for s in range(A_HEADS):
            sl = slice(s * LANES, (s + 1) * LANES)
            qa_ref[rows, sl] = _rope_a(aq[:, sl], cosa, sina).astype(BF16)
        akv = _dot(h, win_ref[:, OFF_AK:OFF_CQ])
        for s in range(A_KV_HEADS):
            sl = slice(s * LANES, (s + 1) * LANES)
            ka_ref[rows, sl] = _rope_a(akv[:, sl], cosa, sina).astype(BF16)
        va_ref[rows, :] = akv[:, A_KV_W:].astype(BF16)

        cq = _dot(h, win_ref[:, OFF_CQ:OFF_CKV])
        cqn = (cq * lax.rsqrt(jnp.mean(cq * cq, axis=-1, keepdims=True) + EPS) * qn_ref[...]).astype(BF16)
        qb = _dot(cqn, wuq_ref[...]) * (B_QK ** -0.5 * LOG2_E)
        for hd in range(B_HEADS):
            lo = hd * B_PAD
            qb_ref[rows, lo:lo + LANES] = qb[:, lo:lo + LANES].astype(BF16)
            qb_ref[rows, lo + LANES:lo + B_PAD] = _rope_b(qb[:, lo + LANES:lo + B_PAD], cosb, sinlo,
                                                          sinhi).astype(BF16)

        ckr = _dot(h, wtail_ref[...])
        ckv = ckr[:, 0:B_KV_RANK]
        ckvn = (ckv * lax.rsqrt(jnp.mean(ckv * ckv, axis=-1, keepdims=True) + EPS) * kvn_ref[...]).astype(BF16)
        kr = _rope_b(ckr[:, B_KV_RANK:], cosb, sinlo, sinhi).astype(BF16)
        kn = _dot(ckvn, wuk_ref[...])
        for hd in range(B_HEADS):
            lo = hd * B_PAD
            kb_ref[rows, lo:lo + LANES] = kn[:, hd * B_NOPE:(hd + 1) * B_NOPE].astype(BF16)
            kb_ref[rows, lo + LANES:lo + B_PAD] = kr
        vt = _dot_nt(wuvt_ref[...], ckvn).astype(BF16)
        ones_row = (lax.broadcasted_iota(jnp.int32, (B_VA - B_V, rb), 0) == 0).astype(F32).astype(BF16)
        for hd in range(B_HEADS):
            vbt_ref[0, hd * B_VA:hd * B_VA + B_V, rows] = vt[hd * B_V:(hd + 1) * B_V]
            vbt_ref[0, hd * B_VA + B_V:(hd + 1) * B_VA, rows] = ones_row


def _inproj(x, mods, row_of_tile, g, weights, tables, table_tile, *, tm):
    t, d = x.shape
    tok = lambda w: pl.BlockSpec((tm, w), lambda i: (i, 0))
    tab = pl.BlockSpec((tm, LANES), lambda i: (table_tile(i), 0))
    widths = (A_Q_W, A_KV_W, A_KV_W, B_HEADS * B_PAD, B_HEADS * B_PAD)
    vt_rows = B_HEADS * B_VA
    return pl.pallas_call(
        _inproj_kernel,
        grid=(t // tm,),
        in_specs=[
            tok(d),
            _const_spec((1, d)),
            _mod_spec(d, row_of_tile, 3),
            _mod_spec(d, row_of_tile, 4),
            *[_const_spec(w.shape) for w in weights],
            tab, tab, tab, tab, tab,
        ],
        out_specs=[tok(w) for w in widths] + [pl.BlockSpec((1, vt_rows, tm), lambda i: (i, 0, 0))],
        out_shape=[jax.ShapeDtypeStruct((t, w), BF16) for w in widths]
        + [jax.ShapeDtypeStruct((t // tm, vt_rows, tm), BF16)],
        compiler_params=_params(("parallel",), 56),
        name="attn_in_proj",
    )(x, g, mods, mods, *weights, *tables)


def _window_kernel(sink_ref, q_ref, k_ref, v_ref, kc_ref, vc_ref, o_ref, qt_ref, *, tq, n):
    n_sub = qt_ref.shape[0]
    for sub in range(n_sub):
        _window_subtile(sink_ref, q_ref.at[0, sub * tq:(sub + 1) * tq, :], k_ref, v_ref, kc_ref, vc_ref,
                        o_ref.at[0, sub * tq:(sub + 1) * tq, :], qt_ref.at[sub],
                        (pl.program_id(1) * n_sub + sub) * tq, tq=tq, n=n)


def _window_subtile(sink_ref, q_ref, k_ref, v_ref, kc_ref, vc_ref, o_ref, qt_ref, q0, *, tq, n):
    span = tq + 2 * A_WINDOW
    start = pl.multiple_of(jnp.clip(q0 - A_WINDOW, 0, n - span), A_WINDOW)
    kpos = start + lax.broadcasted_iota(jnp.int32, (span, tq), 0)
    qpos = q0 + lax.broadcasted_iota(jnp.int32, (span, tq), 1)
    nc = kc_ref.shape[1]
    bias = jnp.concatenate([jnp.where(jnp.abs(kpos - qpos) <= A_WINDOW, 0.0, NEG_BIG), jnp.zeros((nc, tq), F32)],
                           axis=0)
    for g in range(A_KV_HEADS):
        gl = slice(g * A_HEAD_DIM, (g + 1) * A_HEAD_DIM)
        heads = [slice((g * A_REP + r) * A_HEAD_DIM, (g * A_REP + r + 1) * A_HEAD_DIM) for r in range(A_REP)]
        k = jnp.concatenate([k_ref[0, pl.ds(start, span), gl], kc_ref[0, :, gl]], axis=0)
        v = jnp.concatenate([v_ref[0, pl.ds(start, span), gl], vc_ref[0, :, gl]], axis=0)
        for r, hl in enumerate(heads):
            qt_ref[:, r * tq:(r + 1) * tq] = q_ref[:, hl].T
        s = _dot(k, qt_ref[...])
        s = jnp.concatenate([s[:, r * tq:(r + 1) * tq] + bias for r in range(A_REP)], axis=1)
        sink = jnp.concatenate([jnp.full((1, tq), sink_ref[g * A_REP + r] * LOG2_E, F32) for r in range(A_REP)],
                               axis=1)
        m = jnp.maximum(jnp.max(s, axis=0, keepdims=True), sink)
        p = jnp.exp2(s - m)
        denom = jnp.sum(p, axis=0, keepdims=True) + jnp.exp2(sink - m)
        o = jnp.transpose(_dot_tn(v, p.astype(BF16)) / denom)
        for r, hl in enumerate(heads):
            o_ref[:, hl] = o[r * tq:(r + 1) * tq].astype(BF16)


def _window_gqa(qa, ka, va, kca, vca, sink, *, tq, n_sub):
    b, n, _ = qa.shape
    nc = kca.shape[1]
    assert n >= tq + 2 * A_WINDOW
    tq_blk = tq * n_sub
    return pl.pallas_call(
        functools.partial(_window_kernel, tq=tq, n=n),
        grid=(b, n // tq_blk),
        in_specs=[
            pl.BlockSpec(memory_space=pltpu.SMEM),
            pl.BlockSpec((1, tq_blk, A_Q_W), lambda bi, qi: (bi, qi, 0)),
            pl.BlockSpec((1, n, A_KV_W), lambda bi, qi: (bi, 0, 0)),
            pl.BlockSpec((1, n, A_KV_W), lambda bi, qi: (bi, 0, 0)),
            pl.BlockSpec((1, nc, A_KV_W), lambda bi, qi: (bi, 0, 0)),
            pl.BlockSpec((1, nc, A_KV_W), lambda bi, qi: (bi, 0, 0)),
        ],
        out_specs=pl.BlockSpec((1, tq_blk, A_Q_W), lambda bi, qi: (bi, qi, 0)),
        out_shape=jax.ShapeDtypeStruct((b, n, A_Q_W), BF16),
        scratch_shapes=[pltpu.VMEM((n_sub, A_HEAD_DIM, A_REP * tq), BF16)],
        compiler_params=_params(("parallel", "arbitrary"), 48),
        name="window_gqa",
    )(sink, qa, ka, va, kca, vca)


def _mla_kernel(q_ref, k_ref, vt_ref, kc_ref, vct_ref, *rest, tk, n_cast):
    cast_src, o_ref, cast_dst = rest[:n_cast], rest[n_cast], rest[n_cast + 1:2 * n_cast + 1]
    sa_ref, sb_ref, qt_ref = rest[2 * n_cast + 1:]
    for src, dst in zip(cast_src, cast_dst):
        dst[...] = src[...].astype(BF16)

    tq = qt_ref.shape[2]
    for sub in range(qt_ref.shape[0]):
        rows = slice(sub * tq, (sub + 1) * tq)
        _mla_subtile(q_ref.at[0, rows, :], k_ref, vt_ref, kc_ref, vct_ref, o_ref.at[0, rows, :],
                     sa_ref.at[sub], sb_ref.at[sub], qt_ref.at[sub], tk=tk)


def _mla_subtile(q_ref, k_ref, vt_ref, kc_ref, vct_ref, o_ref, sa_ref, sb_ref, qt_ref, *, tk):
    qt_ref[...] = q_ref[...].T
    qt = qt_ref[...]
    nchunks = vt_ref.shape[1]

    def stage_scores(buf, c):
        s = _dot(k_ref[0, pl.ds(pl.multiple_of(c * tk, tk), tk), :], qt)
        buf[...] = s
        return jnp.max(s, axis=0, keepdims=True)

    def probs(s, m):
        return jnp.exp2(s - m).astype(BF16)

    def update(s, s_max, vt, m_old, acc_old):
        m = jnp.maximum(m_old, s_max)
        acc = jnp.exp2(m_old - m) * acc_old + _dot(vt, probs(s, m))
        return m, acc

    bufs = (sa_ref, sb_ref)
    next_max = stage_scores(bufs[0], 0)
    s = _dot(kc_ref[0], qt)
    m = jnp.max(s, axis=0, keepdims=True)
    carry = (m, _dot(vct_ref[0, 0], probs(s, m)), next_max)

    def group(c0, carry, last):
        m, acc, cur_max = carry
        for j in range(MLA_GROUP):
            next_max = cur_max
            if not (last and j == MLA_GROUP - 1):
                next_max = stage_scores(bufs[(j + 1) % 2], c0 + j + 1)
            m, acc = update(bufs[j % 2][...], cur_max, vt_ref[0, c0 + j], m, acc)
            cur_max = next_max
        return m, acc, cur_max

    carry = lax.fori_loop(0, nchunks // MLA_GROUP - 1,
                          lambda i, cr: group(i * MLA_GROUP, cr, False), carry)
    _, acc, _ = group(nchunks - MLA_GROUP, carry, True)
    o_ref[...] = jnp.transpose(acc[0:B_V] / acc[B_V:B_V + 1]).astype(BF16)


def _cast_split(rows, cols, steps):
    for cb in range(1, steps + 1):
        rb = steps // cb
        if (steps % cb == 0 and rows % rb == 0 and cols % cb == 0
                and (rows // rb) % BF16_SUBLANES == 0 and (cols // cb) % LANES == 0):
            return rb, cb
    raise ValueError(f"cannot split a ({rows}, {cols}) matrix into {steps} aligned blocks")


def _mla_attention(qb, kb, vbt, kcb, vcbt, casts, *, tq, n_sub):
    b, n, _ = qb.shape
    nc = kcb.shape[1]
    nchunks, _, tk = vbt.shape[1:]
    tq_blk = tq * n_sub
    nq = n // tq_blk
    assert nchunks * tk == n and nchunks % MLA_GROUP == 0 and vcbt.shape[1] == 1 and vcbt.shape[3] == nc
    steps = b * B_HEADS * nq

    cast_in, cast_out, cast_shapes = [], [], []
    for w, layer in casts:
        _, rows, cols = w.shape
        rb, cb = _cast_split(rows, cols, steps)
        blk = (rows // rb, cols // cb)
        step = lambda bi, hi, qi: (bi * B_HEADS + hi) * nq + qi
        cast_in.append(pl.BlockSpec((None,) + blk,
                                    lambda bi, hi, qi, layer=layer, cb=cb: (layer, step(bi, hi, qi) // cb,
                                                                            step(bi, hi, qi) % cb)))
        cast_out.append(pl.BlockSpec(blk, lambda bi, hi, qi, cb=cb: (step(bi, hi, qi) // cb, step(bi, hi, qi) % cb)))
        cast_shapes.append(jax.ShapeDtypeStruct((rows, cols), BF16))

    out = pl.pallas_call(
        functools.partial(_mla_kernel, tk=tk, n_cast=len(casts)),
        grid=(b, B_HEADS, nq),
        in_specs=[
            pl.BlockSpec((1, tq_blk, B_PAD), lambda bi, hi, qi: (bi, qi, hi)),
            pl.BlockSpec((1, n, B_PAD), lambda bi, hi, qi: (bi, 0, hi)),
            pl.BlockSpec((1, nchunks, B_VA, tk), lambda bi, hi, qi: (bi, 0, hi, 0)),
            pl.BlockSpec((1, nc, B_PAD), lambda bi, hi, qi: (bi, 0, hi)),
            pl.BlockSpec((1, 1, B_VA, nc), lambda bi, hi, qi: (bi, 0, hi, 0)),
        ] + cast_in,
        out_specs=[pl.BlockSpec((1, tq_blk, B_V), lambda bi, hi, qi: (bi, qi, hi))] + cast_out,
        out_shape=[jax.ShapeDtypeStruct((b, n, B_HEADS * B_V), BF16)] + cast_shapes,
        scratch_shapes=[pltpu.VMEM((n_sub, tk, tq), F32), pltpu.VMEM((n_sub, tk, tq), F32),
                        pltpu.VMEM((n_sub, B_PAD, tq), BF16)],
        compiler_params=_params(("arbitrary", "arbitrary", "arbitrary"), 56),
        name="mla_attention",
    )(qb, kb, vbt, kcb, vcbt, *[w for w, _ in casts])
    return out[0], out[1:]


def _outproj_kernel(x_ref, oa_ref, ob_ref, w_ref, gt_ref, o_ref):
    ka = oa_ref.shape[1]
    y = _dot(oa_ref[...], w_ref[0:ka, :]) + _dot(ob_ref[...], w_ref[ka:, :])
    o_ref[...] = x_ref[...] + gt_ref[0] * y


def _outproj(x, oa, ob, w, mods, row_of_tile, *, tm):
    t, d = x.shape
    return pl.pallas_call(
        _outproj_kernel,
        grid=(t // tm,),
        in_specs=[
            pl.BlockSpec((tm, d), lambda i: (i, 0)),
            pl.BlockSpec((tm, oa.shape[1]), lambda i: (i, 0)),
            pl.BlockSpec((tm, ob.shape[1]), lambda i: (i, 0)),
            _const_spec(w.shape),
            _mod_spec(d, row_of_tile, 5),
        ],
        out_specs=pl.BlockSpec((tm, d), lambda i: (i, 0)),
        out_shape=jax.ShapeDtypeStruct((t, d), F32),
        compiler_params=_params(("parallel",), 48),
        name="attn_out_proj",
    )(x, oa, ob, w, mods)


def _pool_kernel(x_ref, xp_ref, xn_ref, g_ref, sh_ref, sc_ref, gt_ref, w_ref, ps_ref, o_ref, hp_ref,
                 *, tm, n):
    i = pl.program_id(1)
    last = pl.num_programs(1) - 1
    g, sh, sc = g_ref[...], sh_ref[0], sc_ref[0]
    x = x_ref[0]
    hp_ref[0:POOL_HALO, :] = jnp.where(i > 0, _rms_mod(xp_ref[0], g, sh, sc), 0.0)
    hp_ref[POOL_HALO:POOL_HALO + tm, :] = _rms_mod(x, g, sh, sc)
    hp_ref[POOL_HALO + tm:, :] = jnp.where(i < last, _rms_mod(xn_ref[0], g, sh, sc), 0.0)

    t = i * tm + lax.broadcasted_iota(jnp.int32, (tm, 1), 0)
    gw = x.shape[1] // len(POOL_WINDOWS)
    for gi, w in enumerate(POOL_WINDOWS):
        cols = slice(gi * gw, (gi + 1) * gw)
        rows = tm + 2 * POOL_HALO
        shift_up = lambda a, k: a if k == 0 else pltpu.roll(a, rows - k, 0)
        tot = shift_up(hp_ref[:, cols], POOL_HALO - w // 2)
        k = 1
        while k < w:
            tot = tot + shift_up(tot, k)
            k *= 2
        tot = tot[0:tm]
        cnt =(jnp.minimum(t - w // 2 + w, n) - jnp.maximum(t - w // 2, 0)).astype(F32)
        diff = tot / cnt - hp_ref[pl.ds(POOL_HALO, tm), cols]
        y = _dot(diff.astype(BF16), w_ref[gi])
        o_ref[0, :, cols] = x[:, cols] + (gt_ref[0][:, cols] * ps_ref[:, cols]) * y


def _pool_mixer(x3, mods, g, w, ps, *, tm):
    b, n, d = x3.shape
    hb = tm // POOL_HALO
    mod = lambda k: pl.BlockSpec((1, 1, d), lambda bi, i: (bi * N_MOD + k, 0, 0))
    return pl.pallas_call(
        functools.partial(_pool_kernel, tm=tm, n=n),
        grid=(b, n // tm),
        in_specs=[
            pl.BlockSpec((1, tm, d), lambda bi, i: (bi, i, 0)),
            pl.BlockSpec((1, POOL_HALO, d), lambda bi, i: (bi, jnp.maximum(i * hb - 1, 0), 0)),
            pl.BlockSpec((1, POOL_HALO, d), lambda bi, i: (bi, jnp.minimum((i + 1) * hb, n // POOL_HALO - 1), 0)),
            pl.BlockSpec((1, d), lambda bi, i: (0, 0)),
            mod(3), mod(4), mod(5),
            pl.BlockSpec(w.shape, lambda bi, i: (0, 0, 0)),
            pl.BlockSpec((1, d), lambda bi, i: (0, 0)),
        ],
        out_specs=pl.BlockSpec((1, tm, d), lambda bi, i: (bi, i, 0)),
        out_shape=jax.ShapeDtypeStruct((b, n, d), F32),
        scratch_shapes=[pltpu.VMEM((tm + 2 * POOL_HALO, d), F32)],
        compiler_params=_params(("parallel", "arbitrary"), 48),
        name="pool_mixer",
    )(x3, x3, x3, g, mods, mods, mods, w, ps)


def _rope_tables(n):
    rows = n // GRID_W
    row = jnp.arange(rows, dtype=F32)
    col = jnp.arange(GRID_W, dtype=F32)

    def cos_sin(rot_dim):
        nf = rot_dim // 4
        inv = ROPE_BASE ** (-jnp.arange(nf, dtype=F32) / nf)
        expand = lambda fn: jnp.concatenate([jnp.repeat(fn(row[:, None] * inv), GRID_W, axis=0),
                                             jnp.tile(fn(col[:, None] * inv), (rows, 1))], axis=-1)
        return expand(jnp.cos), expand(jnp.sin)

    ca, sa = cos_sin(A_HEAD_DIM)
    cb, sb = cos_sin(B_ROPE)
    zb = jnp.zeros_like(cb)
    zpad = jnp.zeros((n, LANES - B_ROPE), F32)
    return (jnp.concatenate([ca, ca], axis=-1), jnp.concatenate([-sa, sa], axis=-1),
            jnp.concatenate([cb, cb, zpad], axis=-1), jnp.concatenate([-sb, zb, zpad], axis=-1),
            jnp.concatenate([zb, sb, zpad], axis=-1))


def _identity_tables(n):
    one = jnp.ones((n, LANES), F32)
    zero = jnp.zeros((n, LANES), F32)
    return (one, zero, one, zero, zero)


def _attn_weights(w_in, q_norm, w_uq, kv_norm, w_ukv):
    d = w_in.shape[0]
    head = w_in[:, :OFF_CKV].astype(BF16)
    tail = jnp.concatenate([w_in[:, OFF_CKV:], jnp.zeros((d, IN_W_PAD - IN_W), w_in.dtype)], axis=1).astype(BF16)
    uq = w_uq.reshape(B_Q_RANK, B_HEADS, B_QK)
    uq = jnp.concatenate([uq, jnp.zeros((B_Q_RANK, B_HEADS, B_PAD - B_QK), uq.dtype)], axis=-1)
    ukv = w_ukv.reshape(B_KV_RANK, B_HEADS, B_NOPE + B_V)
    uk = ukv[:, :, :B_NOPE].reshape(B_KV_RANK, B_HEADS * B_NOPE)
    uvt = ukv[:, :, B_NOPE:].reshape(B_KV_RANK, B_HEADS * B_V).T
    return (head, tail, q_norm.reshape(1, -1), uq.reshape(B_Q_RANK, B_HEADS * B_PAD).astype(BF16),
            kv_norm.reshape(1, -1), uk.astype(BF16), uvt.astype(BF16))


def kernel(x, c, ctx, c_ctx, w_ada, b_ada, norm_ffn1, norm_mix, norm_ffn2, ffn1_w_gate, ffn1_w_up, ffn1_w_down,
           ffn2_w_gate, ffn2_w_up, ffn2_w_down, attn_w_in, attn_sink, mla_q_norm, mla_w_uq, mla_kv_norm, mla_w_ukv,
           attn_w_out, pool_w, pool_scale, final_norm):
    b, n, d = x.shape
    nc = ctx.shape[1]
    depth = w_ada.shape[0]
    f = ffn1_w_gate.shape[2]
    t = b * n

    tm = min(512, n)
    tm_ffn = min(1024, n)
    tf = min(512, f)
    tmc = min(512, b * nc)
    tq_a = min(256, n)
    tq_b = min(1024, n)
    assert n % tm == 0 and f % tf == 0 and (b * nc) % tmc == 0 and n % GRID_W == 0

    tiles_per_sample = n // tm
    x_row = lambda i: i // tiles_per_sample
    ffn_row = lambda i: i // (n // tm_ffn)
    ctx_row = lambda i: b

    mods = _mods(c, c_ctx, w_ada, b_ada)
    rope = _rope_tables(n)
    fn = final_norm.reshape(1, d)

    ffn_f32 = ((ffn1_w_gate, ffn1_w_up, ffn1_w_down), (ffn2_w_gate, ffn2_w_up, ffn2_w_down))
    ffn_bf16 = {}

    def ffn_weights(family, layer):
        if (family, layer) not in ffn_bf16:
            ffn_bf16[(family, layer)] = tuple(w[layer].astype(BF16) for w in ffn_f32[family])
        return ffn_bf16[(family, layer)]

    xs = x.reshape(t, d)
    cx = ctx.reshape(b * nc, d)
    for l in range(depth):
        is_attn = l % 2 == 0
        ctx_out = any(j % 2 == 0 for j in range(l + 1, depth))
        ctx_in = is_attn or ctx_out
        i = l // 2
        m = mods[l]
        g1, gm, g2 = (v[l].reshape(1, d) for v in (norm_ffn1, norm_mix, norm_ffn2))
        w1 = ffn_weights(0, l)

        xs = _ffn(xs, m, 0, ffn_row, g1, *w1, fn, tm=tm_ffn, tf=tf, final_norm=False)
        if ctx_in:
            cx = _ffn(cx, m, 0, ctx_row, g1, *w1, fn, tm=tmc, tf=tf, final_norm=False)

        if is_attn:
            aw = _attn_weights(attn_w_in[i], mla_q_norm[i], mla_w_uq[i], mla_kv_norm[i], mla_w_ukv[i])
            qa, ka, va, qb, kb, vbt = _inproj(xs, m, x_row, gm, aw, rope, lambda ti: ti % tiles_per_sample, tm=tm)
            _, kca, vca, _, kcb, vcbt = _inproj(cx, m, ctx_row, gm, aw, _identity_tables(nc), lambda ti: 0, tm=nc)
            r3 = lambda a, rows: a.reshape(b, rows, a.shape[-1])
            r4 = lambda a: a.reshape(b, a.shape[0] // b, a.shape[1], a.shape[2])
            oa = _window_gqa(r3(qa, n), r3(ka, n), r3(va, n), r3(kca, nc), r3(vca, nc), attn_sink[i], tq=tq_a,
                             n_sub=min(WINDOW_SUBTILES, n // tq_a))
            pending = [(fam, lj) for lj in range(l, depth) for fam in (0, 1) if (lj, fam) > (l, 0)]
            ob, cast = _mla_attention(r3(qb, n), r3(kb, n), r4(vbt), r3(kcb, nc), r4(vcbt),
                                      [(w, lj) for fam, lj in pending for w in ffn_f32[fam]] + [(attn_w_out, i)],
                                      tq=tq_b, n_sub=min(MLA_SUBTILES, n // tq_b))
            for k, key in enumerate(pending):
                ffn_bf16[key] = tuple(cast[3 * k:3 * k + 3])
            xs = _outproj(xs, oa.reshape(t, -1), ob.reshape(t, -1), cast[-1], m, x_row, tm=tm)
            if ctx_out:
                raise NotImplementedError("context-stream attention output is not needed at this depth")
        else:
            xs = _pool_mixer(xs.reshape(b, n, d), m, gm, pool_w[i].astype(BF16), pool_scale[i].reshape(1, d),
                             tm=tm).reshape(t, d)
            if ctx_out:
                raise NotImplementedError("context-stream pooling output is not needed at this depth")

        w2 = ffn_weights(1, l)
        xs = _ffn(xs, m, 6, ffn_row, g2, *w2, fn, tm=tm_ffn, tf=tf, final_norm=(l == depth - 1))
        if ctx_out:
            cx = _ffn(cx, m, 6, ctx_row, g2, *w2, fn, tm=tmc, tf=tf, final_norm=False)
    return xs.reshape(b, n, d)
```
